```python
import jax, jax.numpy as jnp
from jax import lax
import numpy as np

D_MODEL = 2048
BATCH = 2
SEQ = 4096
DEPTH = 2
DEC_BATCH = 128
DEC_SEQ = 1
PAST_LEN = 16384
PAGE_SIZE = 128

N_MIXERS = 2
N_A_LAYERS = (DEPTH + 1) // 2
N_B_LAYERS = DEPTH // 2
A_HEADS = 16
A_HEAD_DIM = D_MODEL // A_HEADS
A_KV_HEADS = 2
A_GROUP = A_HEADS // A_KV_HEADS
A_WIDTH = A_HEADS * A_HEAD_DIM
A_KV_WIDTH = A_KV_HEADS * A_HEAD_DIM
IDX_HEADS = 16
IDX_DIM = 64
TOPK_MAX = 256
A_SPLITS = (A_WIDTH, A_KV_WIDTH, A_KV_WIDTH, IDX_HEADS * IDX_DIM, IDX_DIM, IDX_HEADS, A_WIDTH)
A_IN = sum(A_SPLITS)
B_HEADS = 16
Q_LORA = 512
KV_LORA = 512
NOPE_DIM = 128
ROPE_DIM = 64
V_DIM = D_MODEL // B_HEADS
B_WIDTH = B_HEADS * V_DIM
B_SPLITS = (Q_LORA, KV_LORA, ROPE_DIM, B_WIDTH)
B_IN = sum(B_SPLITS)
MLA_SCALE = (NOPE_DIM + ROPE_DIM) ** -0.5

ROPE_THETA = 10000.0
EPS = 1e-6
Q_BLOCK = 128

kernel_name = "hybrid_dsa_mla_decode_step"


def _split(z, sizes):
    outs, off = [], 0
    for s in sizes:
        outs.append(z[..., off:off + s])
        off += s
    return outs


def rmsnorm(x, g):
    xf = x.astype(jnp.float32)
    y = xf * lax.rsqrt(jnp.mean(xf * xf, axis=-1, keepdims=True) + EPS)
    return (y * g.astype(jnp.float32)).astype(x.dtype)


def rope(x, pos):
    half = x.shape[-1] // 2
    inv_freq = ROPE_THETA ** (-jnp.arange(half, dtype=jnp.float32) / half)
    ang = pos.astype(jnp.float32)[:, None] * inv_freq[None, :]
    cos = jnp.cos(ang)[:, None, :]
    sin = jnp.sin(ang)[:, None, :]
    xf = x.astype(jnp.float32)
    x1, x2 = xf[..., :half], xf[..., half:]
    return jnp.concatenate([x1 * cos - x2 * sin, x2 * cos + x1 * sin], axis=-1).astype(x.dtype)


def gated_out(o, gate, w_out):
    return (o * jax.nn.silu(gate)) @ w_out


def dsa_project(h, w_in, pos):
    bsz, t = h.shape[:2]
    q, k, v, iq, ik, iw, gate = _split(h @ w_in, A_SPLITS)
    q = rope(q.reshape(bsz, t, A_HEADS, A_HEAD_DIM), pos)
    k = rope(k.reshape(bsz, t, A_KV_HEADS, A_HEAD_DIM), pos)
    v = v.reshape(bsz, t, A_KV_HEADS, A_HEAD_DIM)
    iq = rope(iq.reshape(bsz, t, IDX_HEADS, IDX_DIM), pos)
    ik = rope(ik[:, :, None, :], pos)[:, :, 0]
    return q, k, v, iq, ik, iw, gate


def index_scores(iq, iw, ik):
    s = jnp.einsum('bqhd,bkd->bqhk', iq, ik, preferred_element_type=jnp.float32) * IDX_DIM ** -0.5
    return jnp.einsum('bqhk,bqh->bqk', jax.nn.relu(s), iw.astype(jnp.float32)) * IDX_HEADS ** -0.5


def gqa_attend(q, k_sel, v_sel, valid):
    bsz, nq = q.shape[:2]
    qg = q.reshape(bsz, nq, A_KV_HEADS, A_GROUP, A_HEAD_DIM)
    s = jnp.einsum('bqgrd,bqkgd->bqgrk', qg, k_sel, preferred_element_type=jnp.float32) * A_HEAD_DIM ** -0.5
    s = jnp.where(valid[:, :, None, None, :], s, -jnp.inf)
    p = jax.nn.softmax(s, axis=-1).astype(v_sel.dtype)
    o = jnp.einsum('bqgrk,bqkgd->bqgrd', p, v_sel)
    return o.reshape(bsz, nq, A_WIDTH)


def dsa_prompt(q, k, v, iq, ik, iw):
    bsz, seq = q.shape[:2]
    topk = min(TOPK_MAX, seq // 4)
    kpos = jnp.arange(seq)
    bidx = jnp.arange(bsz)[:, None, None]

    def block(i):
        q0 = i * Q_BLOCK
        qb = lax.dynamic_slice_in_dim(q, q0, Q_BLOCK, axis=1)
        iqb = lax.dynamic_slice_in_dim(iq, q0, Q_BLOCK, axis=1)
        iwb = lax.dynamic_slice_in_dim(iw, q0, Q_BLOCK, axis=1)
        qpos = q0 + jnp.arange(Q_BLOCK)
        sc = index_scores(iqb, iwb, ik)
        sc = jnp.where(kpos[None, None, :] <= qpos[None, :, None], sc, -jnp.inf)
        _, sel = lax.top_k(sc, topk)
        valid = sel <= qpos[None, :, None]
        return gqa_attend(qb, k[bidx, sel], v[bidx, sel], valid)

    out = lax.map(block, jnp.arange(seq // Q_BLOCK))
    return jnp.swapaxes(out, 0, 1).reshape(bsz, seq, A_WIDTH)


def dsa_sample(q, k_new, v_new, iq, iw, ik_new, page_table, cache_k, cache_v, cache_idx, layer):
    t = q.shape[1]
    topk = min(TOPK_MAX, (PAST_LEN + t) // 4)
    qpos = PAST_LEN + jnp.arange(t)
    kpos = jnp.arange(PAST_LEN + t)

    def one(args):
        qs, ks, vs, iqs, iws, iks, pt = args
        ik_all = jnp.concatenate([cache_idx[layer, pt].reshape(PAST_LEN, IDX_DIM), iks], axis=0)
        sc = index_scores(iqs[None], iws[None], ik_all[None])[0]
        sc = jnp.where(kpos[None, :] <= qpos[:, None], sc, -jnp.inf)
        _, sel = lax.top_k(sc, topk)
        valid = sel <= qpos[:, None]
        in_past = (sel < PAST_LEN)[..., None, None]
        ps = jnp.minimum(sel, PAST_LEN - 1)
        phys = pt[ps // PAGE_SIZE]
        slot = ps % PAGE_SIZE
        new_i = jnp.clip(sel - PAST_LEN, 0, t - 1)
        k_sel = jnp.where(in_past, cache_k[layer, phys, slot], ks[new_i])
        v_sel = jnp.where(in_past, cache_v[layer, phys, slot], vs[new_i])
        return gqa_attend(qs[None], k_sel[None], v_sel[None], valid[None])[0]

    return lax.map(one, (q, k_new, v_new, iq, iw, ik_new, page_table))


def mla_project(h, w_in, q_a_norm, w_q_b, kv_a_norm, pos):
    bsz, t = h.shape[:2]
    cq, ckv, kpe, gate = _split(h @ w_in, B_SPLITS)
    q = (rmsnorm(cq, q_a_norm) @ w_q_b).reshape(bsz, t, B_HEADS, NOPE_DIM + ROPE_DIM)
    q_nope = q[..., :NOPE_DIM]
    q_pe = rope(q[..., NOPE_DIM:], pos)
    ckv = rmsnorm(ckv, kv_a_norm)
    kpe = rope(kpe[:, :, None, :], pos)[:, :, 0]
    return q_nope, q_pe, ckv, kpe, gate


def mla_prompt(q_nope, q_pe, ckv, kpe, w_kv_b):
    bsz, seq = q_nope.shape[:2]
    kv = (ckv @ w_kv_b).reshape(bsz, seq, B_HEADS, NOPE_DIM + V_DIM)
    k_nope, v = kv[..., :NOPE_DIM], kv[..., NOPE_DIM:]
    kpos = jnp.arange(seq)

    def block(i):
        q0 = i * Q_BLOCK
        qn = lax.dynamic_slice_in_dim(q_nope, q0, Q_BLOCK, axis=1)
        qp = lax.dynamic_slice_in_dim(q_pe, q0, Q_BLOCK, axis=1)
        qpos = q0 + jnp.arange(Q_BLOCK)
        s = (jnp.einsum('bqhd,bkhd->bhqk', qn, k_nope, preferred_element_type=jnp.float32)
             + jnp.einsum('bqhr,bkr->bhqk', qp, kpe, preferred_element_type=jnp.float32)) * MLA_SCALE
        s = jnp.where(kpos[None, None, None, :] <= qpos[None, None, :, None], s, -jnp.inf)
        p = jax.nn.softmax(s, axis=-1).astype(v.dtype)
        return jnp.einsum('bhqk,bkhd->bqhd', p, v).reshape(bsz, Q_BLOCK, B_WIDTH)

    out = lax.map(block, jnp.arange(seq // Q_BLOCK))
    return jnp.swapaxes(out, 0, 1).reshape(bsz, seq, B_WIDTH)


def mla_sample(q_nope, q_pe, ckv_new, kpe_new, w_kv_b, page_table, cache_ckv, cache_kpe, layer):
    bsz, t = q_nope.shape[:2]
    w = w_kv_b.reshape(KV_LORA, B_HEADS, NOPE_DIM + V_DIM)
    w_uk, w_uv = w[..., :NOPE_DIM], w[..., NOPE_DIM:]
    q_lat = jnp.einsum('bthd,chd->bthc', q_nope, w_uk)
    qpos = PAST_LEN + jnp.arange(t)
    kpos = jnp.arange(PAST_LEN + t)
    mask = kpos[None, None, :] <= qpos[:, None, None]

    def one(args):
        ql, qp, cn, kn, pt = args
        c_all = jnp.concatenate([cache_ckv[layer, pt].reshape(PAST_LEN, KV_LORA), cn], axis=0)
        k_all = jnp.concatenate([cache_kpe[layer, pt].reshape(PAST_LEN, ROPE_DIM), kn], axis=0)
        s = (jnp.einsum('thc,lc->thl', ql, c_all, preferred_element_type=jnp.float32)
             + jnp.einsum('thr,lr->thl', qp, k_all, preferred_element_type=jnp.float32)) * MLA_SCALE
        s = jnp.where(mask, s, -jnp.inf)
        p = jax.nn.softmax(s, axis=-1).astype(c_all.dtype)
        return jnp.einsum('thl,lc->thc', p, c_all)

    o_lat = lax.map(one, (q_lat, q_pe, ckv_new, kpe_new, page_table))
    return jnp.einsum('bthc,chd->bthd', o_lat, w_uv).reshape(bsz, t, B_WIDTH)


def setup_inputs(seed: int = 0) -> dict:
    key = jax.random.key(seed)
    ks = jax.random.split(key, 24)
    f32 = jnp.float32
    n_pages = PAST_LEN // PAGE_SIZE
    n_pool = (DEC_BATCH * n_pages * 5) // 4

    def wgt(k, shape, fan_in):
        return jax.random.normal(k, shape, f32) * fan_in ** -0.5

    def gain(k, shape):
        return 1.0 + 0.02 * jax.random.normal(k, shape, f32)

    page_table = jax.random.permutation(ks[0], n_pool)[: DEC_BATCH * n_pages].reshape(DEC_BATCH, n_pages).astype(jnp.int32)
    return {
        "x_prompt": jax.random.normal(ks[1], (BATCH, SEQ, D_MODEL), f32),
        "x_sample": jax.random.normal(ks[2], (DEC_BATCH, DEC_SEQ, D_MODEL), f32),
        "cache_a_k": jax.random.normal(ks[3], (N_A_LAYERS, n_pool, PAGE_SIZE, A_KV_HEADS, A_HEAD_DIM), f32),
        "cache_a_v": jax.random.normal(ks[4], (N_A_LAYERS, n_pool, PAGE_SIZE, A_KV_HEADS, A_HEAD_DIM), f32),
        "cache_a_idx": jax.random.normal(ks[5], (N_A_LAYERS, n_pool, PAGE_SIZE, IDX_DIM), f32),
        "cache_b_ckv": jax.random.normal(ks[6], (N_B_LAYERS, n_pool, PAGE_SIZE, KV_LORA), f32),
        "cache_b_kpe": jax.random.normal(ks[7], (N_B_LAYERS, n_pool, PAGE_SIZE, ROPE_DIM), f32),
        "page_table": page_table,
        "norm_a": gain(ks[8], (N_A_LAYERS, D_MODEL)),
        "w_in_a": wgt(ks[9], (N_A_LAYERS, D_MODEL, A_IN), D_MODEL),
        "w_out_a": wgt(ks[10], (N_A_LAYERS, A_WIDTH, D_MODEL), A_WIDTH),
        "norm_b": gain(ks[11], (N_B_LAYERS, D_MODEL)),
        "w_in_b": wgt(ks[12], (N_B_LAYERS, D_MODEL, B_IN), D_MODEL),
        "q_a_norm_b": gain(ks[13], (N_B_LAYERS, Q_LORA)),
        "w_q_b": wgt(ks[14], (N_B_LAYERS, Q_LORA, B_HEADS * (NOPE_DIM + ROPE_DIM)), Q_LORA),
        "kv_a_norm_b": gain(ks[15], (N_B_LAYERS, KV_LORA)),
        "w_kv_b": wgt(ks[16], (N_B_LAYERS, KV_LORA, B_HEADS * (NOPE_DIM + V_DIM)), KV_LORA),
        "w_out_b": wgt(ks[17], (N_B_LAYERS, B_WIDTH, D_MODEL), B_WIDTH),
        "final_norm": gain(ks[18], (D_MODEL,)),
    }


def reference(x_prompt, x_sample, cache_a_k, cache_a_v, cache_a_idx, cache_b_ckv, cache_b_kpe, page_table,
              norm_a, w_in_a, w_out_a, norm_b, w_in_b, q_a_norm_b, w_q_b, kv_a_norm_b, w_kv_b, w_out_b, final_norm):
    pos_p = jnp.arange(SEQ, dtype=jnp.int32)
    pos_s = PAST_LEN + jnp.arange(DEC_SEQ, dtype=jnp.int32)
    xp, xs = x_prompt, x_sample
    pa_k, pa_v, pa_idx, sa_k, sa_v, sa_idx = [], [], [], [], [], []
    pb_ckv, pb_kpe, sb_ckv, sb_kpe = [], [], [], []

    for i in range(DEPTH):
        j = i // N_MIXERS
        if i % N_MIXERS == 0:
            q, k, v, iq, ik, iw, g = dsa_project(rmsnorm(xp, norm_a[j]), w_in_a[j], pos_p)
            xp = xp + gated_out(dsa_prompt(q, k, v, iq, ik, iw), g, w_out_a[j])
            pa_k.append(k); pa_v.append(v); pa_idx.append(ik)
            q, k, v, iq, ik, iw, g = dsa_project(rmsnorm(xs, norm_a[j]), w_in_a[j], pos_s)
            o = dsa_sample(q, k, v, iq, iw, ik, page_table, cache_a_k, cache_a_v, cache_a_idx, j)
            xs = xs + gated_out(o, g, w_out_a[j])
            sa_k.append(k); sa_v.append(v); sa_idx.append(ik)
        else:
            qn, qp, ckv, kpe, g = mla_project(rmsnorm(xp, norm_b[j]), w_in_b[j], q_a_norm_b[j], w_q_b[j], kv_a_norm_b[j], pos_p)
            xp = xp + gated_out(mla_prompt(qn, qp, ckv, kpe, w_kv_b[j]), g, w_out_b[j])
            pb_ckv.append(ckv); pb_kpe.append(kpe)
            qn, qp, ckv, kpe, g = mla_project(rmsnorm(xs, norm_b[j]), w_in_b[j], q_a_norm_b[j], w_q_b[j], kv_a_norm_b[j], pos_s)
            o = mla_sample(qn, qp, ckv, kpe, w_kv_b[j], page_table, cache_b_ckv, cache_b_kpe, j)
            xs = xs + gated_out(o, g, w_out_b[j])
            sb_ckv.append(ckv); sb_kpe.append(kpe)

    y_prompt = rmsnorm(xp, final_norm)
    y_sample = rmsnorm(xs, final_norm)
    return (y_prompt, y_sample,
            jnp.stack(pa_k), jnp.stack(pa_v), jnp.stack(pa_idx), jnp.stack(pb_ckv), jnp.stack(pb_kpe),
            jnp.stack(sa_k), jnp.stack(sa_v), jnp.stack(sa_idx), jnp.stack(sb_ckv), jnp.stack(sb_kpe))
```

```python
import functools

import jax
import jax.numpy as jnp
from jax import lax
from jax.experimental import pallas as pl
from jax.experimental.pallas import tpu as pltpu

F32 = jnp.float32
BF16 = jnp.bfloat16
I32 = jnp.int32

LANE = 128
VMEM_LIMIT = 56 * 1024 * 1024
NEG = -1e30
INT_MIN = -(2 ** 31)

A_HEADS = 16
A_KV_HEADS = 2
A_GROUP = A_HEADS // A_KV_HEADS
IDX_HEADS = 16
IDX_DIM = 64
TOPK_MAX = 256
B_HEADS = 16
Q_LORA = 512
KV_LORA = 512
NOPE_DIM = 128
ROPE_DIM = 64
MLA_SCALE = (NOPE_DIM + ROPE_DIM) ** -0.5
ROPE_THETA = 10000.0
EPS = 1e-6

_NT = (((1,), (1,)), ((), ()))


def _params(*sem):
    return pltpu.CompilerParams(dimension_semantics=sem, vmem_limit_bytes=VMEM_LIMIT)


def _silu(x):
    return x * (1.0 / (1.0 + jnp.exp(-x)))


def _rmsnorm_kernel(x_ref, g_ref, o_ref):
    x = x_ref[...]
    y = x * lax.rsqrt(jnp.mean(x * x, axis=-1, keepdims=True) + EPS)
    o_ref[...] = (y * g_ref[...]).astype(o_ref.dtype)


def rmsnorm_cast(x, g, tm):
    m, d = x.shape
    return pl.pallas_call(
        _rmsnorm_kernel,
        grid=(m // tm,),
        in_specs=[pl.BlockSpec((tm, d), lambda i: (i, 0)),
                  pl.BlockSpec((1, d), lambda i: (0, 0))],
        out_specs=pl.BlockSpec((tm, d), lambda i: (i, 0)),
        out_shape=jax.ShapeDtypeStruct((m, d), BF16),
        compiler_params=_params("arbitrary"),
        name="rmsnorm_cast",
    )(x, g.reshape(1, d))


def _rope_slab(y, cosf, sinf, head_dim):
    if head_dim == LANE:
        swapped = pltpu.roll(y, LANE // 2, 1)
    else:
        lane = lax.broadcasted_iota(I32, y.shape, 1)
        first_half = (lane % head_dim) < (head_dim // 2)
        swapped = jnp.where(first_half,
                            pltpu.roll(y, LANE - head_dim // 2, 1),
                            pltpu.roll(y, head_dim // 2, 1))
    return y * cosf + swapped * sinf


def _mm_kernel(*refs, mode, head_dim, rope_slabs, n_slabs):
    h_ref, w_ref = refs[0], refs[1]
    acc = jnp.dot(h_ref[...], w_ref[...], preferred_element_type=F32)
    if mode == "rope":
        cosf, sinf = refs[2][...], refs[3][...]
        outs = refs[4:]
        slabs = []
        for s in range(n_slabs):
            y = acc[:, s * LANE:(s + 1) * LANE]
            if rope_slabs == "all" or s % 2 == 1:
                y = _rope_slab(y, cosf, sinf, head_dim)
            slabs.append(y)
        acc = slabs[0] if n_slabs == 1 else jnp.concatenate(slabs, axis=1)
    elif mode == "rmsnorm":
        g = refs[2][...]
        outs = refs[3:]
        acc = acc * lax.rsqrt(jnp.mean(acc * acc, axis=-1, keepdims=True) + EPS) * g
    else:
        outs = refs[2:]
    for o_ref in outs:
        o_ref[...] = acc.astype(o_ref.dtype)


def matmul_epi(h, w, out_dtypes, *, tm, tn, mode="plain", tables=None, head_dim=LANE,
               rope_slabs="all", gain=None, name="proj"):
    m, k = h.shape
    n = w.shape[1]
    tm = min(tm, m)
    tn = min(tn, n)
    in_specs = [pl.BlockSpec((tm, k), lambda j, i: (i, 0)),
                pl.BlockSpec((k, tn), lambda j, i: (0, j))]
    args = [h, w]
    if mode == "rope":
        cosf, sinf = tables
        t_blocks = cosf.shape[0] // tm
        tab_spec = pl.BlockSpec((tm, LANE), lambda j, i: (i % t_blocks, 0))
        in_specs += [tab_spec, tab_spec]
        args += [cosf, sinf]
    elif mode == "rmsnorm":
        assert tn == n
        in_specs.append(pl.BlockSpec((1, n), lambda j, i: (0, 0)))
        args.append(gain.reshape(1, n))
    outs = pl.pallas_call(
        functools.partial(_mm_kernel, mode=mode, head_dim=head_dim, rope_slabs=rope_slabs,
                          n_slabs=tn // LANE),
        grid=(n // tn, m // tm),
        in_specs=in_specs,
        out_specs=[pl.BlockSpec((tm, tn), lambda j, i: (i, j)) for _ in out_dtypes],
        out_shape=[jax.ShapeDtypeStruct((m, n), dt) for dt in out_dtypes],
        compiler_params=_params("arbitrary", "arbitrary"),
        name=name,
    )(*args)
    return outs


def _outproj_kernel(og_ref, w_ref, x_ref, g_ref, *outs, final):
    y = x_ref[...] + jnp.dot(og_ref[...], w_ref[...], preferred_element_type=F32)
    yn = y * lax.rsqrt(jnp.mean(y * y, axis=-1, keepdims=True) + EPS) * g_ref[...]
    if final:
        outs[0][...] = yn
    else:
        outs[0][...] = y
        outs[1][...] = yn.astype(outs[1].dtype)


def outproj_residual_norm(og, w, x, gain, *, tm, final):
    m, k = og.shape
    d = w.shape[1]
    tm = min(tm, m)
    row = lambda i: (i, 0)
    out_shape = ([jax.ShapeDtypeStruct((m, d), F32)] if final else
                 [jax.ShapeDtypeStruct((m, d), F32), jax.ShapeDtypeStruct((m, d), BF16)])
    return pl.pallas_call(
        functools.partial(_outproj_kernel, final=final),
        grid=(m // tm,),
        in_specs=[pl.BlockSpec((tm, k), row),
                  pl.BlockSpec((k, d), lambda i: (0, 0)),
                  pl.BlockSpec((tm, d), row),
                  pl.BlockSpec((1, d), lambda i: (0, 0))],
        out_specs=[pl.BlockSpec((tm, d), row) for _ in out_shape],
        out_shape=out_shape,
        compiler_params=_params("arbitrary"),
        name="outproj",
    )(og, w, x, gain.reshape(1, d))


def _sortable_key(x):
    b = lax.bitcast_convert_type(x, I32)
    key = b ^ ((b >> 31) & I32(0x7FFFFFFF))
    return jnp.where(key == -1, 0, key)


def _count(keys_ref, n_chunks, chunk, rows, pred):
    def body(c, cnt):
        c0 = pl.multiple_of(c * chunk, chunk)
        for s in range(chunk // LANE):
            kk = keys_ref[:, pl.ds(c0 + s * LANE, LANE)]
            cnt = cnt + jnp.where(pred(kk, c0 + s * LANE), 1.0, 0.0)
        return cnt
    cnt = lax.fori_loop(0, n_chunks, body, jnp.zeros((rows, LANE), F32))
    return jnp.sum(cnt, axis=1, keepdims=True)


def _topk_search(keys_ref, n_chunks, chunk, rows, topk, idx_bits, extra_key=None):
    lane = lax.broadcasted_iota(I32, (rows, LANE), 1)
    bcast = lambda v: jnp.broadcast_to(v, (rows, LANE))

    def t_body(it, t):
        cand = t + lax.shift_left(I32(1), I32(31) - it)
        cb = bcast(cand)
        cnt = _count(keys_ref, n_chunks, chunk, rows, lambda kk, c0: kk >= cb)
        if extra_key is not None:
            cnt = cnt + jnp.where(extra_key >= cand, 1.0, 0.0)
        return jnp.where(cnt >= topk, cand, t)

    t = lax.fori_loop(0, 32, t_body, jnp.full((rows, 1), INT_MIN, I32))
    tb = bcast(t)
    c_gt = _count(keys_ref, n_chunks, chunk, rows, lambda kk, c0: kk > tb)
    if extra_key is not None:
        c_gt = c_gt + jnp.where(extra_key > t, 1.0, 0.0)
    need = topk - c_gt

    def x_body(it, x):
        cand = x + lax.shift_left(I32(1), I32(idx_bits - 1) - it)
        cb = bcast(cand)
        ties_before = _count(keys_ref, n_chunks, chunk, rows,
                             lambda kk, c0: (kk == tb) & ((lane + c0) < cb))
        return jnp.where(ties_before < need, cand, x)

    x = lax.fori_loop(0, idx_bits, x_body, jnp.zeros((rows, 1), I32))
    extra_sel = None
    if extra_key is not None:
        c_eq = _count(keys_ref, n_chunks, chunk, rows, lambda kk, c0: kk == tb)
        extra_sel = (extra_key > t) | ((extra_key == t) & (c_eq < need))
    return t, x, extra_sel


def _dsa_prompt_kernel(iq_ref, ikw_q_ref, ikw_all_ref, q_ref, k_ref, v_ref, gate_ref, o_ref,
                       iklo_ref, ikhi_ref, keys_ref, m_ref, l_ref, acc_ref,
                       *, tq, chunk, topk, idx_bits):
    i = pl.program_id(1)
    head_dim = LANE

    @pl.when(i == 0)
    def _():
        x = ikw_all_ref[...]
        lane = lax.broadcasted_iota(I32, x.shape, 1)
        iklo_ref[...] = jnp.where(lane < IDX_DIM, x, 0.0).astype(BF16)
        ikhi_ref[...] = jnp.where(lane >= IDX_DIM, pltpu.roll(x, IDX_DIM, 1), 0.0).astype(BF16)

    q0 = i * tq
    n_chunks = (q0 + tq + chunk - 1) // chunk
    w = ikw_q_ref[:, IDX_DIM:IDX_DIM + IDX_HEADS] * (IDX_DIM ** -0.5 * IDX_HEADS ** -0.5)

    def score_chunk(c, carry):
        c0 = pl.multiple_of(c * chunk, chunk)
        klo = iklo_ref[pl.ds(c0, chunk), :]
        khi = ikhi_ref[pl.ds(c0, chunk), :]
        acc = jnp.zeros((tq, chunk), F32)
        for j in range(IDX_HEADS // 2):
            lhs = iq_ref[:, j * LANE:(j + 1) * LANE]
            s_lo = lax.dot_general(lhs, klo, _NT, preferred_element_type=F32)
            s_hi = lax.dot_general(lhs, khi, _NT, preferred_element_type=F32)
            acc = acc + jnp.maximum(s_lo, 0.0) * w[:, 2 * j:2 * j + 1]
            acc = acc + jnp.maximum(s_hi, 0.0) * w[:, 2 * j + 1:2 * j + 2]
        col = c0 + lax.broadcasted_iota(I32, (tq, chunk), 1)
        row = q0 + lax.broadcasted_iota(I32, (tq, chunk), 0)
        keys_ref[:, pl.ds(c0, chunk)] = jnp.where(col <= row, _sortable_key(acc), INT_MIN)
        return carry

    lax.fori_loop(0, n_chunks, score_chunk, 0)

    t, x, _ = _topk_search(keys_ref, n_chunks, chunk, tq, topk, idx_bits)
    tb = jnp.broadcast_to(t, (tq, chunk))
    xb = jnp.broadcast_to(x, (tq, chunk))
    scale = head_dim ** -0.5

    for g in range(A_KV_HEADS):
        qg = jnp.concatenate(
            [q_ref[:, (g * A_GROUP + r) * LANE:(g * A_GROUP + r + 1) * LANE] for r in range(A_GROUP)],
            axis=0)
        m_ref[...] = jnp.full(m_ref.shape, NEG, F32)
        l_ref[...] = jnp.zeros(l_ref.shape, F32)
        acc_ref[...] = jnp.zeros(acc_ref.shape, F32)

        def attend_chunk(c, carry):
            c0 = pl.multiple_of(c * chunk, chunk)
            kk = keys_ref[:, pl.ds(c0, chunk)]
            col = c0 + lax.broadcasted_iota(I32, (tq, chunk), 1)
            row = q0 + lax.broadcasted_iota(I32, (tq, chunk), 0)
            sel = ((kk > tb) | ((kk == tb) & (col <= xb))) & (col <= row)
            bias = jnp.where(sel, 0.0, NEG)
            kc = k_ref[pl.ds(c0, chunk), g * LANE:(g + 1) * LANE]
            vc = v_ref[pl.ds(c0, chunk), g * LANE:(g + 1) * LANE]
            s = lax.dot_general(qg, kc, _NT, preferred_element_type=F32) * scale
            s = (s.reshape(A_GROUP, tq, chunk) + bias[None]).reshape(A_GROUP * tq, chunk)
            m_prev = m_ref[...]
            m_new = jnp.maximum(m_prev, jnp.max(s, axis=1, keepdims=True))
            alpha = jnp.exp(m_prev - m_new)
            p = jnp.exp(s - m_new)
            l_ref[...] = alpha * l_ref[...] + jnp.sum(p, axis=1, keepdims=True)
            acc_ref[...] = alpha * acc_ref[...] + jnp.dot(p.astype(BF16), vc, preferred_element_type=F32)
            m_ref[...] = m_new
            return carry

        lax.fori_loop(0, n_chunks, attend_chunk, 0)
        o = acc_ref[...] / l_ref[...]
        for r in range(A_GROUP):
            sl = slice((g * A_GROUP + r) * LANE, (g * A_GROUP + r + 1) * LANE)
            gate = gate_ref[:, sl].astype(F32)
            o_ref[:, sl] = (o[r * tq:(r + 1) * tq] * _silu(gate)).astype(o_ref.dtype)


def dsa_prompt_attention(iq, ikw, q, k, v, gate, *, batch, seq, tq, chunk):
    m = batch * seq
    nq = seq // tq
    topk = min(TOPK_MAX, seq // 4)
    idx_bits = (seq - 1).bit_length()
    qrow = lambda b, i: (b * nq + i, 0)
    brow = lambda b, i: (b, 0)
    kern = functools.partial(_dsa_prompt_kernel, tq=tq, chunk=chunk, topk=topk, idx_bits=idx_bits)
    return pl.pallas_call(
        kern,
        grid=(batch, nq),
        in_specs=[pl.BlockSpec((tq, iq.shape[1]), qrow),
                  pl.BlockSpec((tq, LANE), qrow),
                  pl.BlockSpec((seq, LANE), brow),
                  pl.BlockSpec((tq, q.shape[1]), qrow),
                  pl.BlockSpec((seq, k.shape[1]), brow),
                  pl.BlockSpec((seq, v.shape[1]), brow),
                  pl.BlockSpec((tq, gate.shape[1]), qrow)],
        out_specs=pl.BlockSpec((tq, q.shape[1]), qrow),
        out_shape=jax.ShapeDtypeStruct((m, q.shape[1]), BF16),
        scratch_shapes=[pltpu.VMEM((seq, LANE), BF16),
                        pltpu.VMEM((seq, LANE), BF16),
                        pltpu.VMEM((tq, seq), I32),
                        pltpu.VMEM((A_GROUP * tq, 1), F32),
                        pltpu.VMEM((A_GROUP * tq, 1), F32),
                        pltpu.VMEM((A_GROUP * tq, LANE), F32)],
        compiler_params=_params("arbitrary", "arbitrary"),
        name="dsa_prompt",
    )(iq, ikw, ikw, q, k, v, gate)


def _mla_prompt_kernel(q_ref, kn_ref, kpe_ref, v_ref, gate_ref, o_ref, m_ref, l_ref, acc_ref, *, tq):
    i = pl.program_id(2)
    q = q_ref[...]
    m_ref[...] = jnp.full(m_ref.shape, NEG, F32)
    l_ref[...] = jnp.zeros(l_ref.shape, F32)
    acc_ref[...] = jnp.zeros(acc_ref.shape, F32)

    def step(j, diagonal):
        k0 = pl.multiple_of(j * tq, tq)
        kc = jnp.concatenate([kn_ref[pl.ds(k0, tq), :], kpe_ref[pl.ds(k0, tq), :]], axis=1)
        s = lax.dot_general(q, kc, _NT, preferred_element_type=F32) * MLA_SCALE
        if diagonal:
            col = lax.broadcasted_iota(I32, s.shape, 1)
            row = lax.broadcasted_iota(I32, s.shape, 0)
            s = jnp.where(col <= row, s, NEG)
        m_prev = m_ref[...]
        m_new = jnp.maximum(m_prev, jnp.max(s, axis=1, keepdims=True))
        alpha = jnp.exp(m_prev - m_new)
        p = jnp.exp(s - m_new)
        l_ref[...] = alpha * l_ref[...] + jnp.sum(p, axis=1, keepdims=True)
        acc_ref[...] = alpha * acc_ref[...] + jnp.dot(p.astype(BF16), v_ref[pl.ds(k0, tq), :],
                                                      preferred_element_type=F32)
        m_ref[...] = m_new

    def body(j, carry):
        step(j, False)
        return carry

    lax.fori_loop(0, i, body, 0)
    step(i, True)
    o = acc_ref[...] / l_ref[...]
    o_ref[...] = (o * _silu(gate_ref[...].astype(F32))).astype(o_ref.dtype)


def mla_prompt_attention(qcat, kv, kpe, gate, *, batch, seq, tq):
    m = batch * seq
    nq = seq // tq
    return pl.pallas_call(
        functools.partial(_mla_prompt_kernel, tq=tq),
        grid=(batch, B_HEADS, nq),
        in_specs=[pl.BlockSpec((tq, 2 * LANE), lambda b, h, i: (b * nq + i, h)),
                  pl.BlockSpec((seq, LANE), lambda b, h, i: (b, 2 * h)),
                  pl.BlockSpec((seq, LANE), lambda b, h, i: (b, 0)),
                  pl.BlockSpec((seq, LANE), lambda b, h, i: (b, 2 * h + 1)),
                  pl.BlockSpec((tq, LANE), lambda b, h, i: (b * nq + i, h))],
        out_specs=pl.BlockSpec((tq, LANE), lambda b, h, i: (b * nq + i, h)),
        out_shape=jax.ShapeDtypeStruct((m, B_HEADS * LANE), BF16),
        scratch_shapes=[pltpu.VMEM((tq, 1), F32), pltpu.VMEM((tq, 1), F32), pltpu.VMEM((tq, LANE), F32)],
        compiler_params=_params("arbitrary", "arbitrary", "arbitrary"),
        name="mla_prompt",
    )(qcat, kv, kpe, kv, gate)


def _page_specs(n, shape, layer):
    def make(t):
        return pl.BlockSpec((None, None) + shape,
                            lambda b, pc, pt, *_: (layer, pt[b, pc * n + t], 0, 0))
    return [make(t) for t in range(n)]


def _sample_scores_kernel(pt_ref, iq_ref, w_ref, iknew_ref, *rest, pg):
    pages = rest[:pg]
    sc_ref, new_ref = rest[pg], rest[pg + 1]
    iq = iq_ref[0]
    w = w_ref[0] * (IDX_DIM ** -0.5 * IDX_HEADS ** -0.5)
    rows = []
    for t in range(pg):
        kp = pages[t][...].astype(BF16)
        s = lax.dot_general(iq, kp, _NT, preferred_element_type=F32)
        rows.append(jnp.sum(jnp.maximum(s, 0.0) * w, axis=0, keepdims=True))
    sc_ref[0] = jnp.concatenate(rows, axis=0)

    @pl.when(pl.program_id(1) == 0)
    def _():
        kn = iknew_ref[0].astype(BF16).astype(F32)
        s = jnp.sum(iq.astype(F32) * kn, axis=1, keepdims=True)
        val = jnp.sum(jnp.maximum(s, 0.0) * w, axis=0, keepdims=True)
        new_ref[0] = jnp.broadcast_to(val, new_ref.shape[1:])


def sample_index_scores(page_table, iq, iw, ik_new, cache_idx, layer, *, pg):
    nb, n_pages = page_table.shape
    page = cache_idx.shape[2]
    return pl.pallas_call(
        functools.partial(_sample_scores_kernel, pg=pg),
        grid_spec=pltpu.PrefetchScalarGridSpec(
            num_scalar_prefetch=1,
            grid=(nb, n_pages // pg),
            in_specs=[pl.BlockSpec((1, IDX_HEADS, IDX_DIM), lambda b, pc, pt: (b, 0, 0)),
                      pl.BlockSpec((1, IDX_HEADS, 1), lambda b, pc, pt: (b, 0, 0)),
                      pl.BlockSpec((1, 1, IDX_DIM), lambda b, pc, pt: (b, 0, 0))]
                     + _page_specs(pg, (page, IDX_DIM), layer),
            out_specs=[pl.BlockSpec((1, pg, page), lambda b, pc, pt: (b, pc, 0)),
                       pl.BlockSpec((1, 8, LANE), lambda b, pc, pt: (b, 0, 0))],
        ),
        out_shape=[jax.ShapeDtypeStruct((nb, n_pages, page), F32),
                   jax.ShapeDtypeStruct((nb, 8, LANE), F32)],
        compiler_params=_params("arbitrary", "arbitrary"),
        name="sample_scores",
    )(page_table, iq, iw, ik_new, *([cache_idx] * pg))


def _sample_topk_kernel(sc_ref, new_ref, t_ref, x_ref, ns_ref, keys_ref, *, chunk, topk, idx_bits):
    rows, length = sc_ref.shape
    n_chunks = length // chunk

    def fill(c, carry):
        c0 = pl.multiple_of(c * chunk, chunk)
        keys_ref[:, pl.ds(c0, chunk)] = _sortable_key(sc_ref[:, pl.ds(c0, chunk)])
        return carry

    lax.fori_loop(0, n_chunks, fill, 0)
    new_key = _sortable_key(new_ref[:, 0:1])
    t, x, new_sel = _topk_search(keys_ref, n_chunks, chunk, rows, topk, idx_bits, extra_key=new_key)
    t_ref[...] = jnp.broadcast_to(t, t_ref.shape)
    x_ref[...] = jnp.broadcast_to(x, x_ref.shape)
    ns_ref[...] = jnp.broadcast_to(jnp.where(new_sel, 1, 0), ns_ref.shape)


def sample_topk_rule(scores, new_score, *, chunk):
    rows, length = scores.shape
    topk = min(TOPK_MAX, (length + 1) // 4)
    idx_bits = (length - 1).bit_length()
    t, x, ns = pl.pallas_call(
        functools.partial(_sample_topk_kernel, chunk=chunk, topk=topk, idx_bits=idx_bits),
        out_shape=[jax.ShapeDtypeStruct((rows, LANE), I32)] * 3,
        scratch_shapes=[pltpu.VMEM((rows, length), I32)],
        compiler_params=pltpu.CompilerParams(vmem_limit_bytes=VMEM_LIMIT),
        name="sample_topk",
    )(scores, new_score)
    return t[:, 0], x[:, 0], ns[:, 0]


def _online_softmax_update(m_ref, l_ref, acc_ref, s_list, pv):
    m_prev = m_ref[...]
    m_new = m_prev
    for s in s_list:
        m_new = jnp.maximum(m_new, jnp.max(s, axis=1, keepdims=True))
    alpha = jnp.exp(m_prev - m_new)
    l_new = alpha * l_ref[...]
    acc = alpha * acc_ref[...]
    for t, s in enumerate(s_list):
        p = jnp.exp(s - m_new)
        l_new = l_new + jnp.sum(p, axis=1, keepdims=True)
        acc = acc + pv(t, p)
    m_ref[...] = m_new
    l_ref[...] = l_new
    acc_ref[...] = acc


def _sample_dsa_attn_kernel(pt_ref, t_ref, x_ref, ns_ref, q_ref, sc_ref, knew_ref, vnew_ref, gate_ref,
                            *rest, pg, page):
    kpages, vpages = rest[:pg], rest[pg:2 * pg]
    o_ref, m_ref, l_ref, acc_ref = rest[2 * pg:]
    b, pc = pl.program_id(0), pl.program_id(1)
    scale = LANE ** -0.5

    @pl.when(pc == 0)
    def _():
        m_ref[...] = jnp.full(m_ref.shape, NEG, F32)
        l_ref[...] = jnp.zeros(l_ref.shape, F32)
        acc_ref[...] = jnp.zeros(acc_ref.shape, F32)

    q = q_ref[0]
    q_groups = [q[g * A_GROUP:(g + 1) * A_GROUP] for g in range(A_KV_HEADS)]
    t, x = t_ref[b], x_ref[b]
    key = _sortable_key(sc_ref[0])
    col = ((pc * pg + lax.broadcasted_iota(I32, key.shape, 0)) * page
           + lax.broadcasted_iota(I32, key.shape, 1))
    bias = jnp.where((key > t) | ((key == t) & (col <= x)), 0.0, NEG)

    def group_cat(f):
        return jnp.concatenate([f(g) for g in range(A_KV_HEADS)], axis=0)

    s_list = []
    for pi in range(pg):
        kp = kpages[pi][...].astype(BF16)
        s = group_cat(lambda g: lax.dot_general(q_groups[g], kp[:, g * LANE:(g + 1) * LANE], _NT,
                                                preferred_element_type=F32))
        s_list.append(s * scale + bias[pi:pi + 1])

    def pv(pi, p):
        vp = vpages[pi][...].astype(BF16)
        pb = p.astype(BF16)
        return group_cat(lambda g: jnp.dot(pb[g * A_GROUP:(g + 1) * A_GROUP], vp[:, g * LANE:(g + 1) * LANE],
                                           preferred_element_type=F32))

    _online_softmax_update(m_ref, l_ref, acc_ref, s_list, pv)

    @pl.when(pc == pl.num_programs(1) - 1)
    def _():
        kn = knew_ref[0].astype(BF16).astype(F32)
        vn = vnew_ref[0].astype(BF16).astype(F32)
        qf = q.astype(F32)
        s_new = group_cat(lambda g: jnp.sum(qf[g * A_GROUP:(g + 1) * A_GROUP] * kn[:, g * LANE:(g + 1) * LANE],
                                            axis=1, keepdims=True))
        s_new = s_new * scale + jnp.where(ns_ref[b] > 0, 0.0, NEG)

        def pv_new(_, p):
            pf = p.astype(BF16).astype(F32)
            return group_cat(lambda g: pf[g * A_GROUP:(g + 1) * A_GROUP] * vn[:, g * LANE:(g + 1) * LANE])

        _online_softmax_update(m_ref, l_ref, acc_ref, [s_new], pv_new)
        o = acc_ref[...] / l_ref[...]
        o_ref[0] = (o * _silu(gate_ref[0].astype(F32))).astype(o_ref.dtype)


def sample_dsa_attention(page_table, t, x, new_sel, q, scores, k_new, v_new, gate, cache_k, cache_v,
                         layer, *, pg):
    nb, n_pages = page_table.shape
    page, width = cache_k.shape[2], cache_k.shape[3]
    per_b = lambda b, pc, *_: (b, 0, 0)
    return pl.pallas_call(
        functools.partial(_sample_dsa_attn_kernel, pg=pg, page=page),
        grid_spec=pltpu.PrefetchScalarGridSpec(
            num_scalar_prefetch=4,
            grid=(nb, n_pages // pg),
            in_specs=[pl.BlockSpec((1, A_HEADS, LANE), per_b),
                      pl.BlockSpec((1, pg, page), lambda b, pc, *_: (b, pc, 0)),
                      pl.BlockSpec((1, 1, width), per_b),
                      pl.BlockSpec((1, 1, width), per_b),
                      pl.BlockSpec((1, A_HEADS, LANE), per_b)]
                     + _page_specs(pg, (page, width), layer) + _page_specs(pg, (page, width), layer),
            out_specs=pl.BlockSpec((1, A_HEADS, LANE), per_b),
            scratch_shapes=[pltpu.VMEM((A_HEADS, 1), F32), pltpu.VMEM((A_HEADS, 1), F32),
                            pltpu.VMEM((A_HEADS, LANE), F32)],
        ),
        out_shape=jax.ShapeDtypeStruct((nb, A_HEADS, LANE), BF16),
        compiler_params=_params("arbitrary", "arbitrary"),
        name="sample_dsa_attn",
    )(page_table, t, x, new_sel, q, scores, k_new, v_new, gate, *([cache_k] * pg), *([cache_v] * pg))


def _head_mm_kernel(x_ref, w_ref, *rest, transpose_w, gated):
    o_ref = rest[-1]
    dims = _NT if transpose_w else (((1,), (0,)), ((), ()))
    y = lax.dot_general(x_ref[...], w_ref[...], dims, preferred_element_type=F32)
    if gated:
        y = y * _silu(rest[0][...].astype(F32))
    o_ref[...] = y.astype(o_ref.dtype)


def absorb_query(qcat, w_kv):
    m = qcat.shape[0]
    c = w_kv.shape[0]
    return pl.pallas_call(
        functools.partial(_head_mm_kernel, transpose_w=True, gated=False),
        grid=(B_HEADS,),
        in_specs=[pl.BlockSpec((m, LANE), lambda h: (0, 2 * h)),
                  pl.BlockSpec((c, LANE), lambda h: (0, 2 * h))],
        out_specs=pl.BlockSpec((m, c), lambda h: (0, h)),
        out_shape=jax.ShapeDtypeStruct((m, B_HEADS * c), BF16),
        compiler_params=_params("arbitrary"),
        name="absorb_query",
    )(qcat, w_kv)


def unabsorb_output(o_lat, w_kv, gate):
    m = o_lat.shape[0]
    c = w_kv.shape[0]
    return pl.pallas_call(
        functools.partial(_head_mm_kernel, transpose_w=False, gated=True),
        grid=(B_HEADS,),
        in_specs=[pl.BlockSpec((m, c), lambda h: (0, h)),
                  pl.BlockSpec((c, LANE), lambda h: (0, 2 * h + 1)),
                  pl.BlockSpec((m, LANE), lambda h: (0, h))],
        out_specs=pl.BlockSpec((m, LANE), lambda h: (0, h)),
        out_shape=jax.ShapeDtypeStruct((m, B_HEADS * LANE), BF16),
        compiler_params=_params("arbitrary"),
        name="unabsorb_output",
    )(o_lat, w_kv, gate)


def _sample_mla_kernel(pt_ref, ql_ref, qp_ref, cnew_ref, knew_ref, *rest, pg):
    cpages, kpages = rest[:pg], rest[pg:2 * pg]
    o_ref, m_ref, l_ref, acc_ref = rest[2 * pg:]
    pc = pl.program_id(1)

    @pl.when(pc == 0)
    def _():
        m_ref[...] = jnp.full(m_ref.shape, NEG, F32)
        l_ref[...] = jnp.zeros(l_ref.shape, F32)
        acc_ref[...] = jnp.zeros(acc_ref.shape, F32)

    ql = ql_ref[0]
    qp = qp_ref[0][:, LANE:LANE + ROPE_DIM]
    c = jnp.concatenate([cp[...] for cp in cpages], axis=0).astype(BF16)
    kpe = jnp.concatenate([kp[...] for kp in kpages], axis=0).astype(BF16)
    s = (lax.dot_general(ql, c, _NT, preferred_element_type=F32)
         + lax.dot_general(qp, kpe, _NT, preferred_element_type=F32)) * MLA_SCALE
    _online_softmax_update(m_ref, l_ref, acc_ref, [s],
                           lambda _, p: jnp.dot(p.astype(BF16), c, preferred_element_type=F32))

    @pl.when(pc == pl.num_programs(1) - 1)
    def _():
        cn = cnew_ref[0].astype(BF16).astype(F32)
        kn = knew_ref[0].astype(BF16).astype(F32)
        s_new = (jnp.sum(ql.astype(F32) * cn, axis=1, keepdims=True)
                 + jnp.sum(qp.astype(F32) * kn, axis=1, keepdims=True)) * MLA_SCALE
        _online_softmax_update(m_ref, l_ref, acc_ref, [s_new],
                               lambda _, p: p.astype(BF16).astype(F32) * cn)
        o_ref[0] = (acc_ref[...] / l_ref[...]).astype(o_ref.dtype)


def sample_mla_attention(page_table, q_lat, qcat, c_new, k_new, cache_ckv, cache_kpe, layer, *, pg):
    nb, n_pages = page_table.shape
    page, c = cache_ckv.shape[2], cache_ckv.shape[3]
    r = cache_kpe.shape[3]
    per_b = lambda b, pc, pt: (b, 0, 0)
    return pl.pallas_call(
        functools.partial(_sample_mla_kernel, pg=pg),
        grid_spec=pltpu.PrefetchScalarGridSpec(
            num_scalar_prefetch=1,
            grid=(nb, n_pages // pg),
            in_specs=[pl.BlockSpec((1, B_HEADS, c), per_b),
                      pl.BlockSpec((1, B_HEADS, 2 * LANE), per_b),
                      pl.BlockSpec((1, 1, c), per_b),
                      pl.BlockSpec((1, 1, r), per_b)]
                     + _page_specs(pg, (page, c), layer) + _page_specs(pg, (page, r), layer),
            out_specs=pl.BlockSpec((1, B_HEADS, c), per_b),
            scratch_shapes=[pltpu.VMEM((B_HEADS, 1), F32), pltpu.VMEM((B_HEADS, 1), F32),
                            pltpu.VMEM((B_HEADS, c), F32)],
        ),
        out_shape=jax.ShapeDtypeStruct((nb, B_HEADS, c), BF16),
        compiler_params=_params("arbitrary", "arbitrary"),
        name="sample_mla_attn",
    )(page_table, q_lat, qcat, c_new, k_new, *([cache_ckv] * pg), *([cache_kpe] * pg))


def _rope_tables(pos, head_dim, identity_upper_half=False):
    half = head_dim // 2
    inv_freq = ROPE_THETA ** (-jnp.arange(half, dtype=F32) / half)
    ang = pos.astype(F32)[:, None] * inv_freq[None, :]
    c, s = jnp.cos(ang), jnp.sin(ang)
    cosf = jnp.concatenate([c, c], axis=1)
    sinf = jnp.concatenate([-s, s], axis=1)
    if head_dim < LANE:
        if identity_upper_half:
            cosf = jnp.concatenate([cosf, jnp.ones_like(cosf)], axis=1)
            sinf = jnp.concatenate([sinf, jnp.zeros_like(sinf)], axis=1)
        else:
            cosf = jnp.concatenate([cosf, cosf], axis=1)
            sinf = jnp.concatenate([sinf, sinf], axis=1)
    return cosf, sinf


def _pad_cols(w, n):
    return jnp.pad(w, ((0, 0), (0, n - w.shape[1])))


def _layer_a_weights(w_in, d):
    a_width = A_HEADS * LANE
    kv_width = A_KV_HEADS * LANE
    sizes = (a_width, kv_width, kv_width, IDX_HEADS * IDX_DIM, IDX_DIM, IDX_HEADS, a_width)
    offs = [0]
    for s in sizes:
        offs.append(offs[-1] + s)
    wq, wk, wv, wiq, wik, wiw, wg = (w_in[:, offs[t]:offs[t + 1]] for t in range(7))
    wikw = _pad_cols(jnp.concatenate([wik, wiw], axis=1), LANE)
    return tuple(w.astype(BF16) for w in (wq, wk, wv, wiq, wikw, wg))


def _layer_b_weights(w_in, w_q_b):
    wcq = w_in[:, :Q_LORA]
    wckv = w_in[:, Q_LORA:Q_LORA + KV_LORA]
    wkpe = _pad_cols(w_in[:, Q_LORA + KV_LORA:Q_LORA + KV_LORA + ROPE_DIM], LANE)
    wg = w_in[:, Q_LORA + KV_LORA + ROPE_DIM:]
    wq = jnp.pad(w_q_b.reshape(Q_LORA, B_HEADS, NOPE_DIM + ROPE_DIM),
                 ((0, 0), (0, 0), (0, 2 * LANE - NOPE_DIM - ROPE_DIM))).reshape(Q_LORA, B_HEADS * 2 * LANE)
    return tuple(w.astype(BF16) for w in (wcq, wckv, wkpe, wg, wq))


def _project_a(h, wa, tabs, tm):
    wq, wk, wv, wiq, wikw, wg = wa
    t128, t64, t64id = tabs
    mm = functools.partial(matmul_epi, h, tm=tm)
    (q,) = mm(wq, [BF16], tn=512, mode="rope", tables=t128, name="a_q")
    k32, k16 = mm(wk, [F32, BF16], tn=256, mode="rope", tables=t128, name="a_k")
    v32, v16 = mm(wv, [F32, BF16], tn=256, name="a_v")
    (iq,) = mm(wiq, [BF16], tn=512, mode="rope", tables=t64, head_dim=IDX_DIM, name="a_iq")
    (ikw,) = mm(wikw, [F32], tn=LANE, mode="rope", tables=t64id, head_dim=IDX_DIM, name="a_ikw")
    (gate,) = mm(wg, [BF16], tn=512, name="a_gate")
    return q, k32, k16, v32, v16, iq, ikw, gate


def _project_b(h, wb, q_a_norm, kv_a_norm, t64id, tm):
    wcq, wckv, wkpe, wg, wq = wb
    mm = functools.partial(matmul_epi, tm=tm)
    (cq,) = mm(h, wcq, [BF16], tn=Q_LORA, mode="rmsnorm", gain=q_a_norm, name="b_cq")
    ckv32, ckv16 = mm(h, wckv, [F32, BF16], tn=KV_LORA, mode="rmsnorm", gain=kv_a_norm, name="b_ckv")
    kpe32, kpe16 = mm(h, wkpe, [F32, BF16], tn=LANE, mode="rope", tables=t64id, head_dim=ROPE_DIM,
                      name="b_kpe")
    (gate,) = mm(h, wg, [BF16], tn=512, name="b_gate")
    (qcat,) = mm(cq, wq, [BF16], tn=512, mode="rope", tables=t64id, head_dim=ROPE_DIM,
                 rope_slabs="odd", name="b_q")
    return qcat, ckv32, ckv16, kpe32, kpe16, gate


def kernel(x_prompt, x_sample, cache_a_k, cache_a_v, cache_a_idx, cache_b_ckv, cache_b_kpe, page_table,
           norm_a, w_in_a, w_out_a, norm_b, w_in_b, q_a_norm_b, w_q_b, kv_a_norm_b, w_kv_b, w_out_b,
           final_norm):
    batch, seq, d = x_prompt.shape
    nb = x_sample.shape[0]
    n_pages = page_table.shape[1]
    page = cache_a_k.shape[2]
    past = n_pages * page
    mp = batch * seq
    tm_p = min(512, seq)
    kv_width = A_KV_HEADS * LANE

    xp = x_prompt.reshape(mp, d)
    xs = x_sample.reshape(nb, d)
    pos_p = jnp.arange(seq, dtype=I32)
    pos_s = jnp.full((nb,), past, dtype=I32)
    tabs_p = (_rope_tables(pos_p, LANE), _rope_tables(pos_p, IDX_DIM), _rope_tables(pos_p, IDX_DIM, True))
    tabs_s = (_rope_tables(pos_s, LANE), _rope_tables(pos_s, IDX_DIM), _rope_tables(pos_s, IDX_DIM, True))

    wa = _layer_a_weights(w_in_a[0], d)
    wb = _layer_b_weights(w_in_b[0], w_q_b[0])
    w_out_a16 = w_out_a[0].astype(BF16)
    w_out_b16 = w_out_b[0].astype(BF16)
    w_kv16 = w_kv_b[0].astype(BF16)

    hp = rmsnorm_cast(xp, norm_a[0], tm_p)
    q, pk32, k16, pv32, v16, iq, p_ikw, gate = _project_a(hp, wa, tabs_p, tm_p)
    og = dsa_prompt_attention(iq, p_ikw, q, k16, v16, gate, batch=batch, seq=seq,
                              tq=min(256, seq), chunk=min(512, seq))
    xp, hp = outproj_residual_norm(og, w_out_a16, xp, norm_b[0], tm=256, final=False)

    hs = rmsnorm_cast(xs, norm_a[0], nb)
    q, sk32, _, sv32, _, iq, s_ikw, gate = _project_a(hs, wa, tabs_s, nb)
    scores, new_score = sample_index_scores(
        page_table, iq.reshape(nb, IDX_HEADS, IDX_DIM),
        s_ikw[:, IDX_DIM:IDX_DIM + IDX_HEADS].reshape(nb, IDX_HEADS, 1),
        s_ikw[:, :IDX_DIM].reshape(nb, 1, IDX_DIM), cache_a_idx, 0, pg=min(32, n_pages))
    t, x, new_sel = sample_topk_rule(scores.reshape(nb, past), new_score[:, 0, :], chunk=min(512, past))
    cache_k2 = cache_a_k.reshape(cache_a_k.shape[:3] + (kv_width,))
    cache_v2 = cache_a_v.reshape(cache_a_v.shape[:3] + (kv_width,))
    og = sample_dsa_attention(page_table, t, x, new_sel, q.reshape(nb, A_HEADS, LANE), scores,
                              sk32.reshape(nb, 1, kv_width), sv32.reshape(nb, 1, kv_width),
                              gate.reshape(nb, A_HEADS, LANE), cache_k2, cache_v2, 0, pg=min(8, n_pages))
    xs, hs = outproj_residual_norm(og.reshape(nb, A_HEADS * LANE), w_out_a16, xs, norm_b[0], tm=nb,
                                   final=False)

    qcat, p_ckv32, ckv16, p_kpe32, kpe16, gate = _project_b(hp, wb, q_a_norm_b[0], kv_a_norm_b[0],
                                                             tabs_p[2], tm_p)
    (kv,) = matmul_epi(ckv16, w_kv16, [BF16], tm=tm_p, tn=512, name="b_kv")
    og = mla_prompt_attention(qcat, kv, kpe16, gate, batch=batch, seq=seq, tq=min(512, seq))
    (y_prompt,) = outproj_residual_norm(og, w_out_b16, xp, final_norm, tm=256, final=True)

    qcat, s_ckv32, _, s_kpe32, _, gate = _project_b(hs, wb, q_a_norm_b[0], kv_a_norm_b[0], tabs_s[2], nb)
    q_lat = absorb_query(qcat, w_kv16)
    o_lat = sample_mla_attention(page_table, q_lat.reshape(nb, B_HEADS, KV_LORA),
                                 qcat.reshape(nb, B_HEADS, 2 * LANE),
                                 s_ckv32.reshape(nb, 1, KV_LORA),
                                 s_kpe32[:, :ROPE_DIM].reshape(nb, 1, ROPE_DIM),
                                 cache_b_ckv, cache_b_kpe, 0, pg=min(8, n_pages))
    og = unabsorb_output(o_lat.reshape(nb, B_HEADS * KV_LORA), w_kv16, gate)
    (y_sample,) = outproj_residual_norm(og, w_out_b16, xs, final_norm, tm=nb, final=True)

    return (y_prompt.reshape(batch, seq, d), y_sample.reshape(nb, 1, d),
            pk32.reshape(1, batch, seq, A_KV_HEADS, LANE), pv32.reshape(1, batch, seq, A_KV_HEADS, LANE),
            p_ikw[:, :IDX_DIM].reshape(1, batch, seq, IDX_DIM),
            p_ckv32.reshape(1, batch, seq, KV_LORA), p_kpe32[:, :ROPE_DIM].reshape(1, batch, seq, ROPE_DIM),
            sk32.reshape(1, nb, 1, A_KV_HEADS, LANE), sv32.reshape(1, nb, 1, A_KV_HEADS, LANE),
            s_ikw[:, :IDX_DIM].reshape(1, nb, 1, IDX_DIM),
            s_ckv32.reshape(1, nb, 1, KV_LORA), s_kpe32[:, :ROPE_DIM].reshape(1, nb, 1, ROPE_DIM))
```

```python
import functools

import jax
import jax.numpy as jnp
from jax import lax
from jax.experimental import pallas as pl
from jax.experimental.pallas import tpu as pltpu

F32 = jnp.float32
BF16 = jnp.bfloat16
I32 = jnp.int32

LANE = 128
VMEM_LIMIT = 56 * 1024 * 1024
NEG = -1e30
INT_MIN = -(2 ** 31)

A_HEADS = 16
A_KV_HEADS = 2
A_GROUP = A_HEADS // A_KV_HEADS
IDX_HEADS = 16
IDX_DIM = 64
TOPK_MAX = 256
B_HEADS = 16
Q_LORA = 512
KV_LORA = 512
NOPE_DIM = 128
ROPE_DIM = 64
MLA_SCALE = (NOPE_DIM + ROPE_DIM) ** -0.5
ROPE_THETA = 10000.0
EPS = 1e-6

_NT = (((1,), (1,)), ((), ()))


def _params(*sem):
    return pltpu.CompilerParams(dimension_semantics=sem, vmem_limit_bytes=VMEM_LIMIT)


def _silu(x):
    return x * (1.0 / (1.0 + jnp.exp(-x)))


def _rmsnorm_kernel(x_ref, g_ref, o_ref):
    x = x_ref[...]
    y = x * lax.rsqrt(jnp.mean(x * x, axis=-1, keepdims=True) + EPS)
    o_ref[...] = (y * g_ref[...]).astype(o_ref.dtype)


def rmsnorm_cast(x, g, tm):
    m, d = x.shape
    return pl.pallas_call(
        _rmsnorm_kernel,
        grid=(m // tm,),
        in_specs=[pl.BlockSpec((tm, d), lambda i: (i, 0)),
                  pl.BlockSpec((1, d), lambda i: (0, 0))],
        out_specs=pl.BlockSpec((tm, d), lambda i: (i, 0)),
        out_shape=jax.ShapeDtypeStruct((m, d), BF16),
        compiler_params=_params("arbitrary"),
        name="rmsnorm_cast",
    )(x, g.reshape(1, d))


def _rope_slab(y, cosf, sinf, head_dim):
    if head_dim == LANE:
        swapped = pltpu.roll(y, LANE // 2, 1)
    else:
        lane = lax.broadcasted_iota(I32, y.shape, 1)
        first_half = (lane % head_dim) < (head_dim // 2)
        swapped = jnp.where(first_half,
                            pltpu.roll(y, LANE - head_dim // 2, 1),
                            pltpu.roll(y, head_dim // 2, 1))
    return y * cosf + swapped * sinf


def _mm_kernel(*refs, mode, head_dim, rope_slabs, n_slabs):
    h_ref, w_ref = refs[0], refs[1]
    acc = jnp.dot(h_ref[...], w_ref[...], preferred_element_type=F32)
    if mode == "rope":
        cosf, sinf = refs[2][...], refs[3][...]
        outs = refs[4:]
        slabs = []
        for s in range(n_slabs):
            y = acc[:, s * LANE:(s + 1) * LANE]
            if rope_slabs == "all" or s % 2 == 1:
                y = _rope_slab(y, cosf, sinf, head_dim)
            slabs.append(y)
        acc = slabs[0] if n_slabs == 1 else jnp.concatenate(slabs, axis=1)
    elif mode == "rmsnorm":
        g = refs[2][...]
        outs = refs[3:]
        acc = acc * lax.rsqrt(jnp.mean(acc * acc, axis=-1, keepdims=True) + EPS) * g
    else:
        outs = refs[2:]
    for o_ref in outs:
        o_ref[...] = acc.astype(o_ref.dtype)


def matmul_epi(h, w, out_dtypes, *, tm, tn, mode="plain", tables=None, head_dim=LANE,
               rope_slabs="all", gain=None, name="proj"):
    m, k = h.shape
    n = w.shape[1]
    tm = min(tm, m)
    tn = min(tn, n)
    in_specs = [pl.BlockSpec((tm, k), lambda j, i: (i, 0)),
                pl.BlockSpec((k, tn), lambda j, i: (0, j))]
    args = [h, w]
    if mode == "rope":
        cosf, sinf = tables
        t_blocks = cosf.shape[0] // tm
        tab_spec = pl.BlockSpec((tm, LANE), lambda j, i: (i % t_blocks, 0))
        in_specs += [tab_spec, tab_spec]
        args += [cosf, sinf]
    elif mode == "rmsnorm":
        assert tn == n
        in_specs.append(pl.BlockSpec((1, n), lambda j, i: (0, 0)))
        args.append(gain.reshape(1, n))
    outs = pl.pallas_call(
        functools.partial(_mm_kernel, mode=mode, head_dim=head_dim, rope_slabs=rope_slabs,
                          n_slabs=tn // LANE),
        grid=(n // tn, m // tm),
        in_specs=in_specs,
        out_specs=[pl.BlockSpec((tm, tn), lambda j, i: (i, j)) for _ in out_dtypes],
        out_shape=[jax.ShapeDtypeStruct((m, n), dt) for dt in out_dtypes],
        compiler_params=_params("arbitrary", "arbitrary"),
        name=name,
    )(*args)
    return outs


def _outproj_kernel(og_ref, w_ref, x_ref, g_ref, *outs, final):
    y = x_ref[...] + jnp.dot(og_ref[...], w_ref[...], preferred_element_type=F32)
    yn = y * lax.rsqrt(jnp.mean(y * y, axis=-1, keepdims=True) + EPS) * g_ref[...]
    if final:
        outs[0][...] = yn
    else:
        outs[0][...] = y
        outs[1][...] = yn.astype(outs[1].dtype)


def outproj_residual_norm(og, w, x, gain, *, tm, final):
    m, k = og.shape
    d = w.shape[1]
    tm = min(tm, m)
    row = lambda i: (i, 0)
    out_shape = ([jax.ShapeDtypeStruct((m, d), F32)] if final else
                 [jax.ShapeDtypeStruct((m, d), F32), jax.ShapeDtypeStruct((m, d), BF16)])
    return pl.pallas_call(
        functools.partial(_outproj_kernel, final=final),
        grid=(m // tm,),
        in_specs=[pl.BlockSpec((tm, k), row),
                  pl.BlockSpec((k, d), lambda i: (0, 0)),
                  pl.BlockSpec((tm, d), row),
                  pl.BlockSpec((1, d), lambda i: (0, 0))],
        out_specs=[pl.BlockSpec((tm, d), row) for _ in out_shape],
        out_shape=out_shape,
        compiler_params=_params("arbitrary"),
        name="outproj",
    )(og, w, x, gain.reshape(1, d))


def _sortable_key(x):
    b = lax.bitcast_convert_type(x, I32)
    key = b ^ ((b >> 31) & I32(0x7FFFFFFF))
    return jnp.where(key == -1, 0, key)


def _count(keys_ref, n_chunks, chunk, rows, pred):
    def body(c, cnt):
        c0 = pl.multiple_of(c * chunk, chunk)
        for s in range(chunk // LANE):
            kk = keys_ref[:, pl.ds(c0 + s * LANE, LANE)]
            cnt = cnt + jnp.where(pred(kk, c0 + s * LANE), 1.0, 0.0)
        return cnt
    cnt = lax.fori_loop(0, n_chunks, body, jnp.zeros((rows, LANE), F32))
    return jnp.sum(cnt, axis=1, keepdims=True)


def _topk_search(keys_ref, n_chunks, chunk, rows, topk, idx_bits, extra_key=None):
    lane = lax.broadcasted_iota(I32, (rows, LANE), 1)
    bcast = lambda v: jnp.broadcast_to(v, (rows, LANE))
    count = functools.partial(_count, keys_ref, n_chunks, chunk, rows)

    def t_body(it, t):
        cand = t + lax.shift_left(I32(1), I32(31) - it)
        cb = bcast(cand)
        cnt = count(lambda kk, c0: kk >= cb)
        if extra_key is not None:
            cnt = cnt + jnp.where(extra_key >= cand, 1.0, 0.0)
        return jnp.where(cnt >= topk, cand, t)

    t = lax.fori_loop(0, 32, t_body, jnp.full((rows, 1), INT_MIN, I32))
    tb = bcast(t)
    c_gt = count(lambda kk, c0: kk > tb)
    c_eq = count(lambda kk, c0: kk == tb)
    c_eq_all = c_eq
    if extra_key is not None:
        c_gt = c_gt + jnp.where(extra_key > t, 1.0, 0.0)
        c_eq_all = c_eq + jnp.where(extra_key == t, 1.0, 0.0)
    need = topk - c_gt

    def tie_search():
        def x_body(it, x):
            cand = x + lax.shift_left(I32(1), I32(idx_bits - 1) - it)
            cb = bcast(cand)
            ties_before = count(lambda kk, c0: (kk == tb) & ((lane + c0) < cb))
            return jnp.where(ties_before < need, cand, x)
        return lax.fori_loop(0, idx_bits, x_body, jnp.zeros((rows, 1), I32))

    has_excess_ties = jnp.max(jnp.where(c_eq_all > need, 1.0, 0.0)) > 0.0
    x = lax.cond(has_excess_ties, tie_search,
                 lambda: jnp.full((rows, 1), 2 ** idx_bits - 1, I32))
    extra_sel = None
    if extra_key is not None:
        extra_sel = (extra_key > t) | ((extra_key == t) & (c_eq < need))
    return t, x, extra_sel


def _dsa_prompt_kernel(iq_ref, ikw_q_ref, ikw_all_ref, q_ref, k_ref, v_ref, gate_ref, o_ref,
                       iklo_ref, ikhi_ref, keys_ref, m_ref, l_ref, acc_ref,
                       *, tq, chunk, topk, idx_bits):
    i = pl.program_id(1)
    head_dim = LANE

    @pl.when(i == 0)
    def _():
        x = ikw_all_ref[...]
        lane = lax.broadcasted_iota(I32, x.shape, 1)
        iklo_ref[...] = jnp.where(lane < IDX_DIM, x, 0.0).astype(BF16)
        ikhi_ref[...] = jnp.where(lane >= IDX_DIM, pltpu.roll(x, IDX_DIM, 1), 0.0).astype(BF16)

    q0 = i * tq
    n_chunks = (q0 + tq + chunk - 1) // chunk
    w = ikw_q_ref[:, IDX_DIM:IDX_DIM + IDX_HEADS] * (IDX_DIM ** -0.5 * IDX_HEADS ** -0.5)

    def score_chunk(c, carry):
        c0 = pl.multiple_of(c * chunk, chunk)
        klo = iklo_ref[pl.ds(c0, chunk), :]
        khi = ikhi_ref[pl.ds(c0, chunk), :]
        acc = jnp.zeros((tq, chunk), F32)
        for j in range(IDX_HEADS // 2):
            lhs = iq_ref[:, j * LANE:(j + 1) * LANE]
            s_lo = lax.dot_general(lhs, klo, _NT, preferred_element_type=F32)
            s_hi = lax.dot_general(lhs, khi, _NT, preferred_element_type=F32)
            acc = acc + jnp.maximum(s_lo, 0.0) * w[:, 2 * j:2 * j + 1]
            acc = acc + jnp.maximum(s_hi, 0.0) * w[:, 2 * j + 1:2 * j + 2]
        col = c0 + lax.broadcasted_iota(I32, (tq, chunk), 1)
        row = q0 + lax.broadcasted_iota(I32, (tq, chunk), 0)
        keys_ref[:, pl.ds(c0, chunk)] = jnp.where(col <= row, _sortable_key(acc), INT_MIN)
        return carry

    lax.fori_loop(0, n_chunks, score_chunk, 0)

    t, x, _ = _topk_search(keys_ref, n_chunks, chunk, tq, topk, idx_bits)
    tb = jnp.broadcast_to(t, (tq, chunk))
    xb = jnp.broadcast_to(x, (tq, chunk))
    scale = head_dim ** -0.5

    for g in range(A_KV_HEADS):
        qg = jnp.concatenate(
            [q_ref[:, (g * A_GROUP + r) * LANE:(g * A_GROUP + r + 1) * LANE] for r in range(A_GROUP)],
            axis=0)
        m_ref[...] = jnp.full(m_ref.shape, NEG, F32)
        l_ref[...] = jnp.zeros(l_ref.shape, F32)
        acc_ref[...] = jnp.zeros(acc_ref.shape, F32)

        def attend_chunk(c, carry):
            c0 = pl.multiple_of(c * chunk, chunk)
            kk = keys_ref[:, pl.ds(c0, chunk)]
            col = c0 + lax.broadcasted_iota(I32, (tq, chunk), 1)
            row = q0 + lax.broadcasted_iota(I32, (tq, chunk), 0)
            sel = ((kk > tb) | ((kk == tb) & (col <= xb))) & (col <= row)
            bias = jnp.where(sel, 0.0, NEG)
            kc = k_ref[pl.ds(c0, chunk), g * LANE:(g + 1) * LANE]
            vc = v_ref[pl.ds(c0, chunk), g * LANE:(g + 1) * LANE]
            s = lax.dot_general(qg, kc, _NT, preferred_element_type=F32) * scale
            s = (s.reshape(A_GROUP, tq, chunk) + bias[None]).reshape(A_GROUP * tq, chunk)
            m_prev = m_ref[...]
            m_new = jnp.maximum(m_prev, jnp.max(s, axis=1, keepdims=True))
            alpha = jnp.exp(m_prev - m_new)
            p = jnp.exp(s - m_new)
            l_ref[...] = alpha * l_ref[...] + jnp.sum(p, axis=1, keepdims=True)
            acc_ref[...] = alpha * acc_ref[...] + jnp.dot(p.astype(BF16), vc, preferred_element_type=F32)
            m_ref[...] = m_new
            return carry

        lax.fori_loop(0, n_chunks, attend_chunk, 0)
        o = acc_ref[...] / l_ref[...]
        for r in range(A_GROUP):
            sl = slice((g * A_GROUP + r) * LANE, (g * A_GROUP + r + 1) * LANE)
            gate = gate_ref[:, sl].astype(F32)
            o_ref[:, sl] = (o[r * tq:(r + 1) * tq] * _silu(gate)).astype(o_ref.dtype)


def dsa_prompt_attention(iq, ikw, q, k, v, gate, *, batch, seq, tq, chunk):
    m = batch * seq
    nq = seq // tq
    topk = min(TOPK_MAX, seq // 4)
    idx_bits = (seq - 1).bit_length()
    qrow = lambda b, i: (b * nq + i, 0)
    brow = lambda b, i: (b, 0)
    kern = functools.partial(_dsa_prompt_kernel, tq=tq, chunk=chunk, topk=topk, idx_bits=idx_bits)
    return pl.pallas_call(
        kern,
        grid=(batch, nq),
        in_specs=[pl.BlockSpec((tq, iq.shape[1]), qrow),
                  pl.BlockSpec((tq, LANE), qrow),
                  pl.BlockSpec((seq, LANE), brow),
                  pl.BlockSpec((tq, q.shape[1]), qrow),
                  pl.BlockSpec((seq, k.shape[1]), brow),
                  pl.BlockSpec((seq, v.shape[1]), brow),
                  pl.BlockSpec((tq, gate.shape[1]), qrow)],
        out_specs=pl.BlockSpec((tq, q.shape[1]), qrow),
        out_shape=jax.ShapeDtypeStruct((m, q.shape[1]), BF16),
        scratch_shapes=[pltpu.VMEM((seq, LANE), BF16),
                        pltpu.VMEM((seq, LANE), BF16),
                        pltpu.VMEM((tq, seq), I32),
                        pltpu.VMEM((A_GROUP * tq, 1), F32),
                        pltpu.VMEM((A_GROUP * tq, 1), F32),
                        pltpu.VMEM((A_GROUP * tq, LANE), F32)],
        compiler_params=_params("arbitrary", "arbitrary"),
        name="dsa_prompt",
    )(iq, ikw, ikw, q, k, v, gate)


MLA_HEADS_PER_STEP = 2


def _mla_prompt_kernel(q_ref, kv_ref, kpe_ref, gate_ref, o_ref, m_ref, l_ref, acc_ref, *, tq):
    i = pl.program_id(2)
    hs = MLA_HEADS_PER_STEP
    m_ref[...] = jnp.full(m_ref.shape, NEG, F32)
    l_ref[...] = jnp.zeros(l_ref.shape, F32)
    acc_ref[...] = jnp.zeros(acc_ref.shape, F32)

    def step(j, diagonal):
        k0 = pl.multiple_of(j * tq, tq)
        kpe = kpe_ref[pl.ds(k0, tq), :]
        for h in range(hs):
            q = q_ref[:, h * 2 * LANE:(h + 1) * 2 * LANE]
            kn = kv_ref[pl.ds(k0, tq), h * 2 * LANE:h * 2 * LANE + LANE]
            v = kv_ref[pl.ds(k0, tq), h * 2 * LANE + LANE:(h + 1) * 2 * LANE]
            kc = jnp.concatenate([kn, kpe], axis=1)
            s = lax.dot_general(q, kc, _NT, preferred_element_type=F32) * MLA_SCALE
            if diagonal:
                col = lax.broadcasted_iota(I32, s.shape, 1)
                row = lax.broadcasted_iota(I32, s.shape, 0)
                s = jnp.where(col <= row, s, NEG)
            m_prev = m_ref[h]
            m_new = jnp.maximum(m_prev, jnp.max(s, axis=1, keepdims=True))
            alpha = jnp.exp(m_prev - m_new)
            p = jnp.exp(s - m_new)
            l_ref[h] = alpha * l_ref[h] + jnp.sum(p, axis=1, keepdims=True)
            acc_ref[h] = alpha * acc_ref[h] + jnp.dot(p.astype(BF16), v, preferred_element_type=F32)
            m_ref[h] = m_new

    def body(j, carry):
        step(j, False)
        return carry

    lax.fori_loop(0, i, body, 0)
    step(i, True)
    for h in range(hs):
        o = acc_ref[h] / l_ref[h]
        sl = slice(h * LANE, (h + 1) * LANE)
        o_ref[:, sl] = (o * _silu(gate_ref[:, sl].astype(F32))).astype(o_ref.dtype)


def mla_prompt_attention(qcat, kv, kpe, gate, *, batch, seq, tq):
    m = batch * seq
    nq = seq // tq
    hs = MLA_HEADS_PER_STEP
    return pl.pallas_call(
        functools.partial(_mla_prompt_kernel, tq=tq),
        grid=(batch, B_HEADS // hs, nq),
        in_specs=[pl.BlockSpec((tq, hs * 2 * LANE), lambda b, h, i: (b * nq + i, h)),
                  pl.BlockSpec((seq, hs * 2 * LANE), lambda b, h, i: (b, h)),
                  pl.BlockSpec((seq, LANE), lambda b, h, i: (b, 0)),
                  pl.BlockSpec((tq, hs * LANE), lambda b, h, i: (b * nq + i, h))],
        out_specs=pl.BlockSpec((tq, hs * LANE), lambda b, h, i: (b * nq + i, h)),
        out_shape=jax.ShapeDtypeStruct((m, B_HEADS * LANE), BF16),
        scratch_shapes=[pltpu.VMEM((hs, tq, 1), F32), pltpu.VMEM((hs, tq, 1), F32),
                        pltpu.VMEM((hs, tq, LANE), F32)],
        compiler_params=_params("arbitrary", "arbitrary", "arbitrary"),
        name="mla_prompt",
    )(qcat, kv, kpe, gate)


def _page_specs(n, shape, layer):
    def make(t):
        return pl.BlockSpec((None, None) + shape,
                            lambda b, pc, pt, *_: (layer, pt[b, pc * n + t], 0, 0))
    return [make(t) for t in range(n)]


def _sample_scores_kernel(pt_ref, iq_ref, w_ref, iknew_ref, *rest, pg, page):
    pages = rest[:pg]
    sc_ref, new_ref = rest[pg], rest[pg + 1]
    iq = iq_ref[0]
    w = w_ref[0] * (IDX_DIM ** -0.5 * IDX_HEADS ** -0.5)
    keys_t = jnp.concatenate([p[...] for p in pages], axis=1).astype(BF16)
    s = jnp.dot(iq, keys_t, preferred_element_type=F32)
    sc = jnp.sum(jnp.maximum(s, 0.0) * w, axis=0, keepdims=True)
    for t in range(pg):
        sc_ref[0, t:t + 1, :] = sc[:, t * page:(t + 1) * page]

    @pl.when(pl.program_id(1) == 0)
    def _():
        kn = iknew_ref[0].astype(BF16).astype(F32)
        s_new = jnp.sum(iq.astype(F32) * kn, axis=1, keepdims=True)
        val = jnp.sum(jnp.maximum(s_new, 0.0) * w, axis=0, keepdims=True)
        new_ref[0] = jnp.broadcast_to(val, new_ref.shape[1:])


def sample_index_scores(page_table, iq, iw, ik_new, cache_idx_t, layer, *, pg):
    nb, n_pages = page_table.shape
    page = cache_idx_t.shape[3]
    return pl.pallas_call(
        functools.partial(_sample_scores_kernel, pg=pg, page=page),
        grid_spec=pltpu.PrefetchScalarGridSpec(
            num_scalar_prefetch=1,
            grid=(nb, n_pages // pg),
            in_specs=[pl.BlockSpec((1, IDX_HEADS, IDX_DIM), lambda b, pc, pt: (b, 0, 0)),
                      pl.BlockSpec((1, IDX_HEADS, 1), lambda b, pc, pt: (b, 0, 0)),
                      pl.BlockSpec((1, 1, IDX_DIM), lambda b, pc, pt: (b, 0, 0))]
                     + _page_specs(pg, (IDX_DIM, page), layer),
            out_specs=[pl.BlockSpec((1, pg, page), lambda b, pc, pt: (b, pc, 0)),
                       pl.BlockSpec((1, 8, LANE), lambda b, pc, pt: (b, 0, 0))],
        ),
        out_shape=[jax.ShapeDtypeStruct((nb, n_pages, page), F32),
                   jax.ShapeDtypeStruct((nb, 8, LANE), F32)],
        compiler_params=_params("arbitrary", "arbitrary"),
        name="sample_scores",
    )(page_table, iq, iw, ik_new, *([cache_idx_t] * pg))


def _sample_topk_kernel(sc_ref, new_ref, t_ref, x_ref, ns_ref, keys_ref, *, chunk, topk, idx_bits):
    rows, length = sc_ref.shape
    n_chunks = length // chunk

    def fill(c, carry):
        c0 = pl.multiple_of(c * chunk, chunk)
        keys_ref[:, pl.ds(c0, chunk)] = _sortable_key(sc_ref[:, pl.ds(c0, chunk)])
        return carry

    lax.fori_loop(0, n_chunks, fill, 0)
    new_key = _sortable_key(new_ref[:, 0:1])
    t, x, new_sel = _topk_search(keys_ref, n_chunks, chunk, rows, topk, idx_bits, extra_key=new_key)
    t_ref[...] = jnp.broadcast_to(t, t_ref.shape)
    x_ref[...] = jnp.broadcast_to(x, x_ref.shape)
    ns_ref[...] = jnp.broadcast_to(jnp.where(new_sel, 1, 0), ns_ref.shape)


def sample_topk_rule(scores, new_score, *, chunk):
    rows, length = scores.shape
    topk = min(TOPK_MAX, (length + 1) // 4)
    idx_bits = (length - 1).bit_length()
    t, x, ns = pl.pallas_call(
        functools.partial(_sample_topk_kernel, chunk=chunk, topk=topk, idx_bits=idx_bits),
        out_shape=[jax.ShapeDtypeStruct((rows, LANE), I32)] * 3,
        scratch_shapes=[pltpu.VMEM((rows, length), I32)],
        compiler_params=pltpu.CompilerParams(vmem_limit_bytes=VMEM_LIMIT),
        name="sample_topk",
    )(scores, new_score)
    return t[:, 0], x[:, 0], ns[:, 0]


def _sample_compact_kernel(t_ref, x_ref, sc_ref, pt_ref, rows_ref, *, topk, page):
    b = pl.program_id(0)
    n_pages = sc_ref.shape[1]
    t, x = t_ref[b], x_ref[b]
    key = _sortable_key(sc_ref[0])
    col = (lax.broadcasted_iota(I32, key.shape, 0) * page + lax.broadcasted_iota(I32, key.shape, 1))
    sel = jnp.where((key > t) | ((key == t) & (col <= x)), 1.0, 0.0).astype(BF16)
    incl = jnp.where(lax.broadcasted_iota(I32, (page, page), 0) <= lax.broadcasted_iota(I32, (page, page), 1),
                     1.0, 0.0).astype(BF16)
    within = jnp.dot(sel, incl, preferred_element_type=F32)
    per_page = lax.dot_general(jnp.ones((8, page), BF16), sel, _NT, preferred_element_type=F32)
    incl_p = incl if n_pages == page else jnp.where(
        lax.broadcasted_iota(I32, (n_pages, n_pages), 0) <= lax.broadcasted_iota(I32, (n_pages, n_pages), 1),
        1.0, 0.0).astype(BF16)
    upto = jnp.dot(per_page.astype(BF16), incl_p, preferred_element_type=F32)
    upto_row = upto[0:1]
    before_row = upto_row - per_page[0:1]
    total = jnp.max(upto_row, axis=1, keepdims=True)

    j = lax.broadcasted_iota(I32, (topk, 1), 0).astype(F32)
    page_of = jnp.sum(jnp.where(upto_row <= j, 1.0, 0.0), axis=1, keepdims=True)
    onehot = jnp.where(lax.broadcasted_iota(I32, (topk, n_pages), 1).astype(F32) == page_of, 1.0, 0.0)
    within_j = jnp.dot(onehot.astype(BF16), within.astype(BF16), preferred_element_type=F32)
    before_j = jnp.sum(onehot * before_row, axis=1, keepdims=True)
    phys_page = jnp.sum(onehot * pt_ref[0], axis=1, keepdims=True)
    slot = jnp.sum(jnp.where(within_j <= j - before_j, 1.0, 0.0), axis=1, keepdims=True)
    rows = jnp.where(j < total, phys_page * page + slot, 0.0)
    rows_ref[0] = rows.astype(I32)


def sample_compact_rows(t, x, scores, page_table, *, topk):
    nb, n_pages, page = scores.shape
    per_b = lambda b, *_: (b, 0, 0)
    rows = pl.pallas_call(
        functools.partial(_sample_compact_kernel, topk=topk, page=page),
        grid_spec=pltpu.PrefetchScalarGridSpec(
            num_scalar_prefetch=2,
            grid=(nb,),
            in_specs=[pl.BlockSpec((1, n_pages, page), per_b),
                      pl.BlockSpec((1, 1, n_pages), per_b)],
            out_specs=pl.BlockSpec((1, topk, 1), per_b),
        ),
        out_shape=jax.ShapeDtypeStruct((nb, topk, 1), I32),
        compiler_params=_params("arbitrary"),
        name="sample_compact",
    )(t, x, scores, page_table.astype(F32).reshape(nb, 1, n_pages))
    return rows.reshape(nb, topk)


def _online_softmax_update(m_ref, l_ref, acc_ref, s_list, pv):
    m_prev = m_ref[...]
    m_new = m_prev
    for s in s_list:
        m_new = jnp.maximum(m_new, jnp.max(s, axis=1, keepdims=True))
    alpha = jnp.exp(m_prev - m_new)
    l_new = alpha * l_ref[...]
    acc = alpha * acc_ref[...]
    for t, s in enumerate(s_list):
        p = jnp.exp(s - m_new)
        l_new = l_new + jnp.sum(p, axis=1, keepdims=True)
        acc = acc + pv(t, p)
    m_ref[...] = m_new
    l_ref[...] = l_new
    acc_ref[...] = acc


GATHER_UNROLL = 8


def _sample_gather_attn_kernel(rows_ref, ns_ref, q_ref, knew_ref, vnew_ref, gate_ref, ck_hbm, cv_hbm,
                               o_ref, kbuf, vbuf, sem, m_ref, l_ref, acc_ref, *, layer, topk):
    b = pl.program_id(0)
    nb = pl.num_programs(0)
    scale = LANE ** -0.5

    def row_copies(bb, slot, j):
        r = rows_ref[bb, j]
        return (pltpu.make_async_copy(ck_hbm.at[layer, r], kbuf.at[slot, j], sem.at[0, slot]),
                pltpu.make_async_copy(cv_hbm.at[layer, r], vbuf.at[slot, j], sem.at[1, slot]))

    def for_all_rows(bb, slot, action):
        def body(jj, carry):
            for u in range(GATHER_UNROLL):
                for cp in row_copies(bb, slot, jj * GATHER_UNROLL + u):
                    action(cp)
            return carry
        lax.fori_loop(0, topk // GATHER_UNROLL, body, 0)

    slot = b % 2

    @pl.when(b == 0)
    def _():
        for_all_rows(0, 0, lambda cp: cp.start())

    @pl.when(b + 1 < nb)
    def _():
        for_all_rows(b + 1, 1 - slot, lambda cp: cp.start())

    for_all_rows(b, slot, lambda cp: cp.wait())

    q = q_ref[0]
    qf = q.astype(F32)
    new_sel = ns_ref[b]
    n_past = topk - new_sel

    def group_cat(f):
        return jnp.concatenate([f(g) for g in range(A_KV_HEADS)], axis=0)

    m_ref[...] = jnp.full(m_ref.shape, NEG, F32)
    l_ref[...] = jnp.zeros(l_ref.shape, F32)
    acc_ref[...] = jnp.zeros(acc_ref.shape, F32)
    kg = [kbuf[slot, :, g, :].astype(BF16) for g in range(A_KV_HEADS)]
    vg = [vbuf[slot, :, g, :].astype(BF16) for g in range(A_KV_HEADS)]
    s = group_cat(lambda g: lax.dot_general(q[g * A_GROUP:(g + 1) * A_GROUP], kg[g], _NT,
                                            preferred_element_type=F32))
    valid = lax.broadcasted_iota(I32, s.shape, 1) < n_past
    s = jnp.where(valid, s * scale, NEG)

    def pv(_, p):
        pb = p.astype(BF16)
        return group_cat(lambda g: jnp.dot(pb[g * A_GROUP:(g + 1) * A_GROUP], vg[g],
                                           preferred_element_type=F32))

    _online_softmax_update(m_ref, l_ref, acc_ref, [s], pv)

    kn = knew_ref[0].astype(BF16).astype(F32)
    vn = vnew_ref[0].astype(BF16).astype(F32)
    s_new = group_cat(lambda g: jnp.sum(qf[g * A_GROUP:(g + 1) * A_GROUP] * kn[:, g * LANE:(g + 1) * LANE],
                                        axis=1, keepdims=True))
    s_new = s_new * scale + jnp.where(new_sel > 0, 0.0, NEG)

    def pv_new(_, p):
        pf = p.astype(BF16).astype(F32)
        return group_cat(lambda g: pf[g * A_GROUP:(g + 1) * A_GROUP] * vn[:, g * LANE:(g + 1) * LANE])

    _online_softmax_update(m_ref, l_ref, acc_ref, [s_new], pv_new)
    o = acc_ref[...] / l_ref[...]
    o_ref[0] = (o * _silu(gate_ref[0].astype(F32))).astype(o_ref.dtype)


def sample_gather_attention(rows, new_sel, q, k_new, v_new, gate, cache_k, cache_v, layer):
    nb, topk = rows.shape
    kvh, hd = cache_k.shape[2], cache_k.shape[3]
    per_b = lambda b, *_: (b, 0, 0)
    return pl.pallas_call(
        functools.partial(_sample_gather_attn_kernel, layer=layer, topk=topk),
        grid_spec=pltpu.PrefetchScalarGridSpec(
            num_scalar_prefetch=2,
            grid=(nb,),
            in_specs=[pl.BlockSpec((1, A_HEADS, LANE), per_b),
                      pl.BlockSpec((1, 1, kvh * hd), per_b),
                      pl.BlockSpec((1, 1, kvh * hd), per_b),
                      pl.BlockSpec((1, A_HEADS, LANE), per_b),
                      pl.BlockSpec(memory_space=pl.ANY),
                      pl.BlockSpec(memory_space=pl.ANY)],
            out_specs=pl.BlockSpec((1, A_HEADS, LANE), per_b),
            scratch_shapes=[pltpu.VMEM((2, topk, kvh, hd), F32),
                            pltpu.VMEM((2, topk, kvh, hd), F32),
                            pltpu.SemaphoreType.DMA((2, 2)),
                            pltpu.VMEM((A_HEADS, 1), F32), pltpu.VMEM((A_HEADS, 1), F32),
                            pltpu.VMEM((A_HEADS, LANE), F32)],
        ),
        out_shape=jax.ShapeDtypeStruct((nb, A_HEADS, LANE), BF16),
        compiler_params=_params("arbitrary"),
        name="sample_gather_attn",
    )(rows, new_sel, q, k_new, v_new, gate, cache_k, cache_v)


def _head_mm_kernel(x_ref, w_ref, *rest, transpose_w, gated):
    o_ref = rest[-1]
    dims = _NT if transpose_w else (((1,), (0,)), ((), ()))
    y = lax.dot_general(x_ref[...], w_ref[...], dims, preferred_element_type=F32)
    if gated:
        y = y * _silu(rest[0][...].astype(F32))
    o_ref[...] = y.astype(o_ref.dtype)


def absorb_query(qcat, w_kv):
    m = qcat.shape[0]
    c = w_kv.shape[0]
    return pl.pallas_call(
        functools.partial(_head_mm_kernel, transpose_w=True, gated=False),
        grid=(B_HEADS,),
        in_specs=[pl.BlockSpec((m, LANE), lambda h: (0, 2 * h)),
                  pl.BlockSpec((c, LANE), lambda h: (0, 2 * h))],
        out_specs=pl.BlockSpec((m, c), lambda h: (0, h)),
        out_shape=jax.ShapeDtypeStruct((m, B_HEADS * c), BF16),
        compiler_params=_params("arbitrary"),
        name="absorb_query",
    )(qcat, w_kv)


def unabsorb_output(o_lat, w_kv, gate):
    m = o_lat.shape[0]
    c = w_kv.shape[0]
    return pl.pallas_call(
        functools.partial(_head_mm_kernel, transpose_w=False, gated=True),
        grid=(B_HEADS,),
        in_specs=[pl.BlockSpec((m, c), lambda h: (0, h)),
                  pl.BlockSpec((c, LANE), lambda h: (0, 2 * h + 1)),
                  pl.BlockSpec((m, LANE), lambda h: (0, h))],
        out_specs=pl.BlockSpec((m, LANE), lambda h: (0, h)),
        out_shape=jax.ShapeDtypeStruct((m, B_HEADS * LANE), BF16),
        compiler_params=_params("arbitrary"),
        name="unabsorb_output",
    )(o_lat, w_kv, gate)


def _sample_mla_kernel(pt_ref, ql_ref, qp_ref, cnew_ref, knew_ref, *rest, pg):
    cpages, kpages = rest[:pg], rest[pg:2 * pg]
    o_ref, m_ref, l_ref, acc_ref = rest[2 * pg:]
    pc = pl.program_id(1)

    @pl.when(pc == 0)
    def _():
        m_ref[...] = jnp.full(m_ref.shape, NEG, F32)
        l_ref[...] = jnp.zeros(l_ref.shape, F32)
        acc_ref[...] = jnp.zeros(acc_ref.shape, F32)

    ql = ql_ref[0]
    qp = qp_ref[0][:, LANE:LANE + ROPE_DIM]
    c = jnp.concatenate([cp[...] for cp in cpages], axis=0).astype(BF16)
    kpe_t = jnp.concatenate([kp[...] for kp in kpages], axis=1).astype(BF16)
    s = (lax.dot_general(ql, c, _NT, preferred_element_type=F32)
         + jnp.dot(qp, kpe_t, preferred_element_type=F32)) * MLA_SCALE
    _online_softmax_update(m_ref, l_ref, acc_ref, [s],
                           lambda _, p: jnp.dot(p.astype(BF16), c, preferred_element_type=F32))

    @pl.when(pc == pl.num_programs(1) - 1)
    def _():
        cn = cnew_ref[0].astype(BF16).astype(F32)
        kn = knew_ref[0].astype(BF16).astype(F32)
        s_new = (jnp.sum(ql.astype(F32) * cn, axis=1, keepdims=True)
                 + jnp.sum(qp.astype(F32) * kn, axis=1, keepdims=True)) * MLA_SCALE
        _online_softmax_update(m_ref, l_ref, acc_ref, [s_new],
                               lambda _, p: p.astype(BF16).astype(F32) * cn)
        o_ref[0] = (acc_ref[...] / l_ref[...]).astype(o_ref.dtype)


def sample_mla_attention(page_table, q_lat, qcat, c_new, k_new, cache_ckv, cache_kpe_t, layer, *, pg):
    nb, n_pages = page_table.shape
    page, c = cache_ckv.shape[2], cache_ckv.shape[3]
    r = cache_kpe_t.shape[2]
    per_b = lambda b, pc, pt: (b, 0, 0)
    return pl.pallas_call(
        functools.partial(_sample_mla_kernel, pg=pg),
        grid_spec=pltpu.PrefetchScalarGridSpec(
            num_scalar_prefetch=1,
            grid=(nb, n_pages // pg),
            in_specs=[pl.BlockSpec((1, B_HEADS, c), per_b),
                      pl.BlockSpec((1, B_HEADS, 2 * LANE), per_b),
                      pl.BlockSpec((1, 1, c), per_b),
                      pl.BlockSpec((1, 1, r), per_b)]
                     + _page_specs(pg, (page, c), layer) + _page_specs(pg, (r, page), layer),
            out_specs=pl.BlockSpec((1, B_HEADS, c), per_b),
            scratch_shapes=[pltpu.VMEM((B_HEADS, 1), F32), pltpu.VMEM((B_HEADS, 1), F32),
                            pltpu.VMEM((B_HEADS, c), F32)],
        ),
        out_shape=jax.ShapeDtypeStruct((nb, B_HEADS, c), BF16),
        compiler_params=_params("arbitrary", "arbitrary"),
        name="sample_mla_attn",
    )(page_table, q_lat, qcat, c_new, k_new, *([cache_ckv] * pg), *([cache_kpe_t] * pg))


def _rope_tables(pos, head_dim, identity_upper_half=False):
    half = head_dim // 2
    inv_freq = ROPE_THETA ** (-jnp.arange(half, dtype=F32) / half)
    ang = pos.astype(F32)[:, None] * inv_freq[None, :]
    c, s = jnp.cos(ang), jnp.sin(ang)
    cosf = jnp.concatenate([c, c], axis=1)
    sinf = jnp.concatenate([-s, s], axis=1)
    if head_dim < LANE:
        if identity_upper_half:
            cosf = jnp.concatenate([cosf, jnp.ones_like(cosf)], axis=1)
            sinf = jnp.concatenate([sinf, jnp.zeros_like(sinf)], axis=1)
        else:
            cosf = jnp.concatenate([cosf, cosf], axis=1)
            sinf = jnp.concatenate([sinf, sinf], axis=1)
    return cosf, sinf


def _pad_cols(w, n):
    return jnp.pad(w, ((0, 0), (0, n - w.shape[1])))


def _layer_a_weights(w_in, d):
    a_width = A_HEADS * LANE
    kv_width = A_KV_HEADS * LANE
    sizes = (a_width, kv_width, kv_width, IDX_HEADS * IDX_DIM, IDX_DIM, IDX_HEADS, a_width)
    offs = [0]
    for s in sizes:
        offs.append(offs[-1] + s)
    wq, wk, wv, wiq, wik, wiw, wg = (w_in[:, offs[t]:offs[t + 1]] for t in range(7))
    wikw = _pad_cols(jnp.concatenate([wik, wiw], axis=1), LANE)
    return tuple(w.astype(BF16) for w in (wq, wk, wv, wiq, wikw, wg))


def _layer_b_weights(w_in, w_q_b):
    wcq = w_in[:, :Q_LORA]
    wckv = w_in[:, Q_LORA:Q_LORA + KV_LORA]
    wkpe = _pad_cols(w_in[:, Q_LORA + KV_LORA:Q_LORA + KV_LORA + ROPE_DIM], LANE)
    wg = w_in[:, Q_LORA + KV_LORA + ROPE_DIM:]
    wq = jnp.pad(w_q_b.reshape(Q_LORA, B_HEADS, NOPE_DIM + ROPE_DIM),
                 ((0, 0), (0, 0), (0, 2 * LANE - NOPE_DIM - ROPE_DIM))).reshape(Q_LORA, B_HEADS * 2 * LANE)
    return tuple(w.astype(BF16) for w in (wcq, wckv, wkpe, wg, wq))


def _project_a(h, wa, tabs, tm):
    wq, wk, wv, wiq, wikw, wg = wa
    t128, t64, t64id = tabs
    mm = functools.partial(matmul_epi, h, tm=tm)
    (q,) = mm(wq, [BF16], tn=512, mode="rope", tables=t128, name="a_q")
    k32, k16 = mm(wk, [F32, BF16], tn=256, mode="rope", tables=t128, name="a_k")
    v32, v16 = mm(wv, [F32, BF16], tn=256, name="a_v")
    (iq,) = mm(wiq, [BF16], tn=512, mode="rope", tables=t64, head_dim=IDX_DIM, name="a_iq")
    (ikw,) = mm(wikw, [F32], tn=LANE, mode="rope", tables=t64id, head_dim=IDX_DIM, name="a_ikw")
    (gate,) = mm(wg, [BF16], tn=512, name="a_gate")
    return q, k32, k16, v32, v16, iq, ikw, gate


def _project_b(h, wb, q_a_norm, kv_a_norm, t64id, tm):
    wcq, wckv, wkpe, wg, wq = wb
    mm = functools.partial(matmul_epi, tm=tm)
    (cq,) = mm(h, wcq, [BF16], tn=Q_LORA, mode="rmsnorm", gain=q_a_norm, name="b_cq")
    ckv32, ckv16 = mm(h, wckv, [F32, BF16], tn=KV_LORA, mode="rmsnorm", gain=kv_a_norm, name="b_ckv")
    kpe32, kpe16 = mm(h, wkpe, [F32, BF16], tn=LANE, mode="rope", tables=t64id, head_dim=ROPE_DIM,
                      name="b_kpe")
    (gate,) = mm(h, wg, [BF16], tn=512, name="b_gate")
    (qcat,) = mm(cq, wq, [BF16], tn=512, mode="rope", tables=t64id, head_dim=ROPE_DIM,
                 rope_slabs="odd", name="b_q")
    return qcat, ckv32, ckv16, kpe32, kpe16, gate


def kernel(x_prompt, x_sample, cache_a_k, cache_a_v, cache_a_idx, cache_b_ckv, cache_b_kpe, page_table,
           norm_a, w_in_a, w_out_a, norm_b, w_in_b, q_a_norm_b, w_q_b, kv_a_norm_b, w_kv_b, w_out_b,
           final_norm):
    batch, seq, d = x_prompt.shape
    nb = x_sample.shape[0]
    n_pages = page_table.shape[1]
    n_layers_a, n_pool, page = cache_a_k.shape[:3]
    past = n_pages * page
    mp = batch * seq
    tm_p = min(512, seq)
    kv_width = A_KV_HEADS * LANE

    xp = x_prompt.reshape(mp, d)
    xs = x_sample.reshape(nb, d)
    pos_p = jnp.arange(seq, dtype=I32)
    pos_s = jnp.full((nb,), past, dtype=I32)
    tabs_p = (_rope_tables(pos_p, LANE), _rope_tables(pos_p, IDX_DIM), _rope_tables(pos_p, IDX_DIM, True))
    tabs_s = (_rope_tables(pos_s, LANE), _rope_tables(pos_s, IDX_DIM), _rope_tables(pos_s, IDX_DIM, True))

    wa = _layer_a_weights(w_in_a[0], d)
    wb = _layer_b_weights(w_in_b[0], w_q_b[0])
    w_out_a16 = w_out_a[0].astype(BF16)
    w_out_b16 = w_out_b[0].astype(BF16)
    w_kv16 = w_kv_b[0].astype(BF16)

    cache_idx_t = jnp.swapaxes(cache_a_idx, 2, 3)
    cache_kpe_t = jnp.swapaxes(cache_b_kpe, 2, 3)
    cache_k_rows = cache_a_k.reshape(n_layers_a, n_pool * page, A_KV_HEADS, LANE)
    cache_v_rows = cache_a_v.reshape(n_layers_a, n_pool * page, A_KV_HEADS, LANE)

    hs = rmsnorm_cast(xs, norm_a[0], nb)
    q, sk32, _, sv32, _, iq, s_ikw, gate = _project_a(hs, wa, tabs_s, nb)
    scores, new_score = sample_index_scores(
        page_table, iq.reshape(nb, IDX_HEADS, IDX_DIM),
        s_ikw[:, IDX_DIM:IDX_DIM + IDX_HEADS].reshape(nb, IDX_HEADS, 1),
        s_ikw[:, :IDX_DIM].reshape(nb, 1, IDX_DIM), cache_idx_t, 0, pg=min(32, n_pages))
    t, x, new_sel = sample_topk_rule(scores.reshape(nb, past), new_score[:, 0, :], chunk=min(512, past))
    rows = sample_compact_rows(t, x, scores, page_table, topk=min(TOPK_MAX, (past + 1) // 4))
    og = sample_gather_attention(rows, new_sel, q.reshape(nb, A_HEADS, LANE),
                                 sk32.reshape(nb, 1, kv_width), sv32.reshape(nb, 1, kv_width),
                                 gate.reshape(nb, A_HEADS, LANE), cache_k_rows, cache_v_rows, 0)
    xs, hs = outproj_residual_norm(og.reshape(nb, A_HEADS * LANE), w_out_a16, xs, norm_b[0], tm=nb,
                                   final=False)

    hp = rmsnorm_cast(xp, norm_a[0], tm_p)
    q, pk32, k16, pv32, v16, iq, p_ikw, gate = _project_a(hp, wa, tabs_p, tm_p)
    og = dsa_prompt_attention(iq, p_ikw, q, k16, v16, gate, batch=batch, seq=seq,
                              tq=min(256, seq), chunk=min(512, seq))
    xp, hp = outproj_residual_norm(og, w_out_a16, xp, norm_b[0], tm=256, final=False)

    qcat, s_ckv32, _, s_kpe32, _, gate = _project_b(hs, wb, q_a_norm_b[0], kv_a_norm_b[0], tabs_s[2], nb)
    q_lat = absorb_query(qcat, w_kv16)
    o_lat = sample_mla_attention(page_table, q_lat.reshape(nb, B_HEADS, KV_LORA),
                                 qcat.reshape(nb, B_HEADS, 2 * LANE),
                                 s_ckv32.reshape(nb, 1, KV_LORA),
                                 s_kpe32[:, :ROPE_DIM].reshape(nb, 1, ROPE_DIM),
                                 cache_b_ckv, cache_kpe_t, 0, pg=min(32, n_pages))
    og = unabsorb_output(o_lat.reshape(nb, B_HEADS * KV_LORA), w_kv16, gate)
    (y_sample,) = outproj_residual_norm(og, w_out_b16, xs, final_norm, tm=nb, final=True)

    qcat, p_ckv32, ckv16, p_kpe32, kpe16, gate = _project_b(hp, wb, q_a_norm_b[0], kv_a_norm_b[0],
                                                             tabs_p[2], tm_p)
    (kv,) = matmul_epi(ckv16, w_kv16, [BF16], tm=tm_p, tn=512, name="b_kv")
    og = mla_prompt_attention(qcat, kv, kpe16, gate, batch=batch, seq=seq, tq=min(512, seq))
    (y_prompt,) = outproj_residual_norm(og, w_out_b16, xp, final_norm, tm=256, final=True)

    return (y_prompt.reshape(batch, seq, d), y_sample.reshape(nb, 1, d),
            pk32.reshape(1, batch, seq, A_KV_HEADS, LANE), pv32.reshape(1, batch, seq, A_KV_HEADS, LANE),
            p_ikw[:, :IDX_DIM].reshape(1, batch, seq, IDX_DIM),
            p_ckv32.reshape(1, batch, seq, KV_LORA), p_kpe32[:, :ROPE_DIM].reshape(1, batch, seq, ROPE_DIM),
            sk32.reshape(1, nb, 1, A_KV_HEADS, LANE), sv32.reshape(1, nb, 1, A_KV_HEADS, LANE),
            s_ikw[:, :IDX_DIM].reshape(1, nb, 1, IDX_DIM),
            s_ckv32.reshape(1, nb, 1, KV_LORA), s_kpe32[:, :ROPE_DIM].reshape(1, nb, 1, ROPE_DIM))
```

```python
import functools

import jax
import jax.numpy as jnp
from jax import lax
from jax.experimental import pallas as pl
from jax.experimental.pallas import tpu as pltpu

F32 = jnp.float32
BF16 = jnp.bfloat16
I32 = jnp.int32

LANE = 128
VMEM_LIMIT = 56 * 1024 * 1024
NEG = -1e30
INT_MIN = -(2 ** 31)

A_HEADS = 16
A_KV_HEADS = 2
A_GROUP = A_HEADS // A_KV_HEADS
IDX_HEADS = 16
IDX_DIM = 64
TOPK_MAX = 256
B_HEADS = 16
Q_LORA = 512
KV_LORA = 512
NOPE_DIM = 128
ROPE_DIM = 64
MLA_SCALE = (NOPE_DIM + ROPE_DIM) ** -0.5
ROPE_THETA = 10000.0
EPS = 1e-6

_NT = (((1,), (1,)), ((), ()))


def _params(*sem):
    return pltpu.CompilerParams(dimension_semantics=sem, vmem_limit_bytes=VMEM_LIMIT)


def _silu(x):
    return x * (1.0 / (1.0 + jnp.exp(-x)))


def _rmsnorm_kernel(x_ref, g_ref, o_ref):
    x = x_ref[...]
    y = x * lax.rsqrt(jnp.mean(x * x, axis=-1, keepdims=True) + EPS)
    o_ref[...] = (y * g_ref[...]).astype(o_ref.dtype)


def rmsnorm_cast(x, g, tm):
    m, d = x.shape
    return pl.pallas_call(
        _rmsnorm_kernel,
        grid=(m // tm,),
        in_specs=[pl.BlockSpec((tm, d), lambda i: (i, 0)),
                  pl.BlockSpec((1, d), lambda i: (0, 0))],
        out_specs=pl.BlockSpec((tm, d), lambda i: (i, 0)),
        out_shape=jax.ShapeDtypeStruct((m, d), BF16),
        compiler_params=_params("arbitrary"),
        name="rmsnorm_cast",
    )(x, g.reshape(1, d))


def _rope_slab(y, cosf, sinf, head_dim):
    if head_dim == LANE:
        swapped = pltpu.roll(y, LANE // 2, 1)
    else:
        lane = lax.broadcasted_iota(I32, y.shape, 1)
        first_half = (lane % head_dim) < (head_dim // 2)
        swapped = jnp.where(first_half,
                            pltpu.roll(y, LANE - head_dim // 2, 1),
                            pltpu.roll(y, head_dim // 2, 1))
    return y * cosf + swapped * sinf


def _mm_kernel(*refs, mode, head_dim, rope_slabs, n_slabs):
    h_ref, w_ref = refs[0], refs[1]
    acc = jnp.dot(h_ref[...], w_ref[...], preferred_element_type=F32)
    if mode == "rope":
        cosf, sinf = refs[2][...], refs[3][...]
        outs = refs[4:]
        slabs = []
        for s in range(n_slabs):
            y = acc[:, s * LANE:(s + 1) * LANE]
            if rope_slabs == "all" or s % 2 == 1:
                y = _rope_slab(y, cosf, sinf, head_dim)
            slabs.append(y)
        acc = slabs[0] if n_slabs == 1 else jnp.concatenate(slabs, axis=1)
    elif mode == "rmsnorm":
        g = refs[2][...]
        outs = refs[3:]
        acc = acc * lax.rsqrt(jnp.mean(acc * acc, axis=-1, keepdims=True) + EPS) * g
    else:
        outs = refs[2:]
    for o_ref in outs:
        o_ref[...] = acc.astype(o_ref.dtype)


def matmul_epi(h, w, out_dtypes, *, tm, tn, mode="plain", tables=None, head_dim=LANE,
               rope_slabs="all", gain=None, name="proj"):
    m, k = h.shape
    n = w.shape[1]
    tm = min(tm, m)
    tn = min(tn, n)
    in_specs = [pl.BlockSpec((tm, k), lambda j, i: (i, 0)),
                pl.BlockSpec((k, tn), lambda j, i: (0, j))]
    args = [h, w]
    if mode == "rope":
        cosf, sinf = tables
        t_blocks = cosf.shape[0] // tm
        tab_spec = pl.BlockSpec((tm, LANE), lambda j, i: (i % t_blocks, 0))
        in_specs += [tab_spec, tab_spec]
        args += [cosf, sinf]
    elif mode == "rmsnorm":
        assert tn == n
        in_specs.append(pl.BlockSpec((1, n), lambda j, i: (0, 0)))
        args.append(gain.reshape(1, n))
    outs = pl.pallas_call(
        functools.partial(_mm_kernel, mode=mode, head_dim=head_dim, rope_slabs=rope_slabs,
                          n_slabs=tn // LANE),
        grid=(n // tn, m // tm),
        in_specs=in_specs,
        out_specs=[pl.BlockSpec((tm, tn), lambda j, i: (i, j)) for _ in out_dtypes],
        out_shape=[jax.ShapeDtypeStruct((m, n), dt) for dt in out_dtypes],
        compiler_params=_params("arbitrary", "arbitrary"),
        name=name,
    )(*args)
    return outs


def _outproj_kernel(og_ref, w_ref, x_ref, g_ref, *outs, final):
    y = x_ref[...] + jnp.dot(og_ref[...], w_ref[...], preferred_element_type=F32)
    yn = y * lax.rsqrt(jnp.mean(y * y, axis=-1, keepdims=True) + EPS) * g_ref[...]
    if final:
        outs[0][...] = yn
    else:
        outs[0][...] = y
        outs[1][...] = yn.astype(outs[1].dtype)


def outproj_residual_norm(og, w, x, gain, *, tm, final):
    m, k = og.shape
    d = w.shape[1]
    tm = min(tm, m)
    row = lambda i: (i, 0)
    out_shape = ([jax.ShapeDtypeStruct((m, d), F32)] if final else
                 [jax.ShapeDtypeStruct((m, d), F32), jax.ShapeDtypeStruct((m, d), BF16)])
    return pl.pallas_call(
        functools.partial(_outproj_kernel, final=final),
        grid=(m // tm,),
        in_specs=[pl.BlockSpec((tm, k), row),
                  pl.BlockSpec((k, d), lambda i: (0, 0)),
                  pl.BlockSpec((tm, d), row),
                  pl.BlockSpec((1, d), lambda i: (0, 0))],
        out_specs=[pl.BlockSpec((tm, d), row) for _ in out_shape],
        out_shape=out_shape,
        compiler_params=_params("arbitrary"),
        name="outproj",
    )(og, w, x, gain.reshape(1, d))


def _sortable_key(x):
    b = lax.bitcast_convert_type(x, I32)
    key = b ^ ((b >> 31) & I32(0x7FFFFFFF))
    return jnp.where(key == -1, 0, key)


COUNT_ROWS = 64


def _count(keys_ref, n_chunks, chunk, rows, make_pred):
    rb = min(COUNT_ROWS, rows)

    def block_count(r0):
        rs = slice(r0, r0 + rb)
        pred = make_pred(rs, rb)

        def body(c, cnt):
            c0 = pl.multiple_of(c * chunk, chunk)
            for s in range(chunk // LANE):
                kk = keys_ref[rs, pl.ds(c0 + s * LANE, LANE)]
                cnt = cnt + jnp.where(pred(kk, c0 + s * LANE), 1.0, 0.0)
            return cnt

        return lax.fori_loop(0, n_chunks, body, jnp.zeros((rb, LANE), F32))

    cnt = jnp.concatenate([block_count(r0) for r0 in range(0, rows, rb)], axis=0)
    return jnp.sum(cnt, axis=1, keepdims=True)


def _topk_search(keys_ref, n_chunks, chunk, rows, topk, idx_bits, extra_key=None):
    count = functools.partial(_count, keys_ref, n_chunks, chunk, rows)

    def compare(op, column):
        def make(rs, rb):
            vb = jnp.broadcast_to(column[rs], (rb, LANE))
            return lambda kk, c0: op(kk, vb)
        return make

    def t_body(it, t):
        cand = t + lax.shift_left(I32(1), I32(31) - it)
        cnt = count(compare(lambda kk, vb: kk >= vb, cand))
        if extra_key is not None:
            cnt = cnt + jnp.where(extra_key >= cand, 1.0, 0.0)
        return jnp.where(cnt >= topk, cand, t)

    t = lax.fori_loop(0, 32, t_body, jnp.full((rows, 1), INT_MIN, I32))
    c_gt = count(compare(lambda kk, vb: kk > vb, t))
    c_eq = count(compare(lambda kk, vb: kk == vb, t))
    c_eq_all = c_eq
    if extra_key is not None:
        c_gt = c_gt + jnp.where(extra_key > t, 1.0, 0.0)
        c_eq_all = c_eq + jnp.where(extra_key == t, 1.0, 0.0)
    need = topk - c_gt

    def tie_search():
        def x_body(it, x):
            cand = x + lax.shift_left(I32(1), I32(idx_bits - 1) - it)

            def make(rs, rb):
                tb = jnp.broadcast_to(t[rs], (rb, LANE))
                cb = jnp.broadcast_to(cand[rs], (rb, LANE))
                lane = lax.broadcasted_iota(I32, (rb, LANE), 1)
                return lambda kk, c0: (kk == tb) & ((lane + c0) < cb)

            ties_before = count(make)
            return jnp.where(ties_before < need, cand, x)
        return lax.fori_loop(0, idx_bits, x_body, jnp.zeros((rows, 1), I32))

    has_excess_ties = jnp.max(jnp.where(c_eq_all > need, 1.0, 0.0)) > 0.0
    x = lax.cond(has_excess_ties, tie_search,
                 lambda: jnp.full((rows, 1), 2 ** idx_bits - 1, I32))
    extra_sel = None
    if extra_key is not None:
        extra_sel = (extra_key > t) | ((extra_key == t) & (c_eq < need))
    return t, x, extra_sel


def _dsa_prompt_kernel(iq_ref, ikw_q_ref, ikw_all_ref, q_ref, k_ref, v_ref, gate_ref, o_ref,
                       iklo_ref, ikhi_ref, keys_ref, m_ref, l_ref, acc_ref,
                       *, tq, chunk, topk, idx_bits):
    i = pl.program_id(1)
    head_dim = LANE

    @pl.when(i == 0)
    def _():
        x = ikw_all_ref[...]
        lane = lax.broadcasted_iota(I32, x.shape, 1)
        iklo_ref[...] = jnp.where(lane < IDX_DIM, x, 0.0).astype(BF16)
        ikhi_ref[...] = jnp.where(lane >= IDX_DIM, pltpu.roll(x, IDX_DIM, 1), 0.0).astype(BF16)

    q0 = i * tq
    n_chunks = (q0 + tq + chunk - 1) // chunk
    w = ikw_q_ref[:, IDX_DIM:IDX_DIM + IDX_HEADS] * (IDX_DIM ** -0.5 * IDX_HEADS ** -0.5)

    def score_chunk(c, carry):
        c0 = pl.multiple_of(c * chunk, chunk)
        klo = iklo_ref[pl.ds(c0, chunk), :]
        khi = ikhi_ref[pl.ds(c0, chunk), :]
        acc = jnp.zeros((tq, chunk), F32)
        for j in range(IDX_HEADS // 2):
            lhs = iq_ref[:, j * LANE:(j + 1) * LANE]
            s_lo = lax.dot_general(lhs, klo, _NT, preferred_element_type=F32)
            s_hi = lax.dot_general(lhs, khi, _NT, preferred_element_type=F32)
            acc = acc + jnp.maximum(s_lo, 0.0) * w[:, 2 * j:2 * j + 1]
            acc = acc + jnp.maximum(s_hi, 0.0) * w[:, 2 * j + 1:2 * j + 2]
        col = c0 + lax.broadcasted_iota(I32, (tq, chunk), 1)
        row = q0 + lax.broadcasted_iota(I32, (tq, chunk), 0)
        keys_ref[:, pl.ds(c0, chunk)] = jnp.where(col <= row, _sortable_key(acc), INT_MIN)
        return carry

    lax.fori_loop(0, n_chunks, score_chunk, 0)

    t, x, _ = _topk_search(keys_ref, n_chunks, chunk, tq, topk, idx_bits)
    tb = jnp.broadcast_to(t, (tq, chunk))
    xb = jnp.broadcast_to(x, (tq, chunk))
    scale = head_dim ** -0.5

    for g in range(A_KV_HEADS):
        qg = jnp.concatenate(
            [q_ref[:, (g * A_GROUP + r) * LANE:(g * A_GROUP + r + 1) * LANE] for r in range(A_GROUP)],
            axis=0)
        m_ref[...] = jnp.full(m_ref.shape, NEG, F32)
        l_ref[...] = jnp.zeros(l_ref.shape, F32)
        acc_ref[...] = jnp.zeros(acc_ref.shape, F32)

        def attend_chunk(c, carry):
            c0 = pl.multiple_of(c * chunk, chunk)
            kk = keys_ref[:, pl.ds(c0, chunk)]
            col = c0 + lax.broadcasted_iota(I32, (tq, chunk), 1)
            row = q0 + lax.broadcasted_iota(I32, (tq, chunk), 0)
            sel = ((kk > tb) | ((kk == tb) & (col <= xb))) & (col <= row)
            bias = jnp.where(sel, 0.0, NEG)
            kc = k_ref[pl.ds(c0, chunk), g * LANE:(g + 1) * LANE]
            vc = v_ref[pl.ds(c0, chunk), g * LANE:(g + 1) * LANE]
            s = lax.dot_general(qg, kc, _NT, preferred_element_type=F32) * scale
            s = (s.reshape(A_GROUP, tq, chunk) + bias[None]).reshape(A_GROUP * tq, chunk)
            m_prev = m_ref[...]
            m_new = jnp.maximum(m_prev, jnp.max(s, axis=1, keepdims=True))
            alpha = jnp.exp(m_prev - m_new)
            p = jnp.exp(s - m_new)
            l_ref[...] = alpha * l_ref[...] + jnp.sum(p, axis=1, keepdims=True)
            acc_ref[...] = alpha * acc_ref[...] + jnp.dot(p.astype(BF16), vc, preferred_element_type=F32)
            m_ref[...] = m_new
            return carry

        lax.fori_loop(0, n_chunks, attend_chunk, 0)
        o = acc_ref[...] / l_ref[...]
        for r in range(A_GROUP):
            sl = slice((g * A_GROUP + r) * LANE, (g * A_GROUP + r + 1) * LANE)
            gate = gate_ref[:, sl].astype(F32)
            o_ref[:, sl] = (o[r * tq:(r + 1) * tq] * _silu(gate)).astype(o_ref.dtype)


def dsa_prompt_attention(iq, ikw, q, k, v, gate, *, batch, seq, tq, chunk):
    m = batch * seq
    nq = seq // tq
    topk = min(TOPK_MAX, seq // 4)
    idx_bits = (seq - 1).bit_length()
    qrow = lambda b, i: (b * nq + i, 0)
    brow = lambda b, i: (b, 0)
    kern = functools.partial(_dsa_prompt_kernel, tq=tq, chunk=chunk, topk=topk, idx_bits=idx_bits)
    return pl.pallas_call(
        kern,
        grid=(batch, nq),
        in_specs=[pl.BlockSpec((tq, iq.shape[1]), qrow),
                  pl.BlockSpec((tq, LANE), qrow),
                  pl.BlockSpec((seq, LANE), brow),
                  pl.BlockSpec((tq, q.shape[1]), qrow),
                  pl.BlockSpec((seq, k.shape[1]), brow),
                  pl.BlockSpec((seq, v.shape[1]), brow),
                  pl.BlockSpec((tq, gate.shape[1]), qrow)],
        out_specs=pl.BlockSpec((tq, q.shape[1]), qrow),
        out_shape=jax.ShapeDtypeStruct((m, q.shape[1]), BF16),
        scratch_shapes=[pltpu.VMEM((seq, LANE), BF16),
                        pltpu.VMEM((seq, LANE), BF16),
                        pltpu.VMEM((tq, seq), I32),
                        pltpu.VMEM((A_GROUP * tq, 1), F32),
                        pltpu.VMEM((A_GROUP * tq, 1), F32),
                        pltpu.VMEM((A_GROUP * tq, LANE), F32)],
        compiler_params=_params("arbitrary", "arbitrary"),
        name="dsa_prompt",
    )(iq, ikw, ikw, q, k, v, gate)


MLA_HEADS_PER_STEP = 2


def _mla_prompt_kernel(q_ref, kv_ref, kpe_ref, gate_ref, o_ref, m_ref, l_ref, acc_ref, *, tq):
    i = pl.program_id(2)
    hs = MLA_HEADS_PER_STEP
    m_ref[...] = jnp.full(m_ref.shape, NEG, F32)
    l_ref[...] = jnp.zeros(l_ref.shape, F32)
    acc_ref[...] = jnp.zeros(acc_ref.shape, F32)

    def step(j, diagonal):
        k0 = pl.multiple_of(j * tq, tq)
        kpe = kpe_ref[pl.ds(k0, tq), :]
        for h in range(hs):
            q = q_ref[:, h * 2 * LANE:(h + 1) * 2 * LANE]
            kn = kv_ref[pl.ds(k0, tq), h * 2 * LANE:h * 2 * LANE + LANE]
            v = kv_ref[pl.ds(k0, tq), h * 2 * LANE + LANE:(h + 1) * 2 * LANE]
            kc = jnp.concatenate([kn, kpe], axis=1)
            s = lax.dot_general(q, kc, _NT, preferred_element_type=F32) * MLA_SCALE
            if diagonal:
                col = lax.broadcasted_iota(I32, s.shape, 1)
                row = lax.broadcasted_iota(I32, s.shape, 0)
                s = jnp.where(col <= row, s, NEG)
            m_prev = m_ref[h]
            m_new = jnp.maximum(m_prev, jnp.max(s, axis=1, keepdims=True))
            alpha = jnp.exp(m_prev - m_new)
            p = jnp.exp(s - m_new)
            l_ref[h] = alpha * l_ref[h] + jnp.sum(p, axis=1, keepdims=True)
            acc_ref[h] = alpha * acc_ref[h] + jnp.dot(p.astype(BF16), v, preferred_element_type=F32)
            m_ref[h] = m_new

    def body(j, carry):
        step(j, False)
        return carry

    lax.fori_loop(0, i, body, 0)
    step(i, True)
    for h in range(hs):
        o = acc_ref[h] / l_ref[h]
        sl = slice(h * LANE, (h + 1) * LANE)
        o_ref[:, sl] = (o * _silu(gate_ref[:, sl].astype(F32))).astype(o_ref.dtype)


def mla_prompt_attention(qcat, kv, kpe, gate, *, batch, seq, tq):
    m = batch * seq
    nq = seq // tq
    hs = MLA_HEADS_PER_STEP
    return pl.pallas_call(
        functools.partial(_mla_prompt_kernel, tq=tq),
        grid=(batch, B_HEADS // hs, nq),
        in_specs=[pl.BlockSpec((tq, hs * 2 * LANE), lambda b, h, i: (b * nq + i, h)),
                  pl.BlockSpec((seq, hs * 2 * LANE), lambda b, h, i: (b, h)),
                  pl.BlockSpec((seq, LANE), lambda b, h, i: (b, 0)),
                  pl.BlockSpec((tq, hs * LANE), lambda b, h, i: (b * nq + i, h))],
        out_specs=pl.BlockSpec((tq, hs * LANE), lambda b, h, i: (b * nq + i, h)),
        out_shape=jax.ShapeDtypeStruct((m, B_HEADS * LANE), BF16),
        scratch_shapes=[pltpu.VMEM((hs, tq, 1), F32), pltpu.VMEM((hs, tq, 1), F32),
                        pltpu.VMEM((hs, tq, LANE), F32)],
        compiler_params=_params("arbitrary", "arbitrary", "arbitrary"),
        name="mla_prompt",
    )(qcat, kv, kpe, gate)


def _page_specs(n, shape, layer):
    def make(t):
        return pl.BlockSpec((None, None) + shape,
                            lambda b, pc, pt, *_: (layer, pt[b, pc * n + t], 0, 0))
    return [make(t) for t in range(n)]


def _sample_scores_kernel(pt_ref, iq_ref, w_ref, iknew_ref, *rest, pg, page):
    pages = rest[:pg]
    sc_ref, new_ref = rest[pg], rest[pg + 1]
    iq = iq_ref[0]
    w = w_ref[0] * (IDX_DIM ** -0.5 * IDX_HEADS ** -0.5)
    keys_t = jnp.concatenate([p[...] for p in pages], axis=1).astype(BF16)
    s = jnp.dot(iq, keys_t, preferred_element_type=F32)
    sc = jnp.sum(jnp.maximum(s, 0.0) * w, axis=0, keepdims=True)
    for t in range(pg):
        sc_ref[0, t:t + 1, :] = sc[:, t * page:(t + 1) * page]

    @pl.when(pl.program_id(1) == 0)
    def _():
        kn = iknew_ref[0].astype(BF16).astype(F32)
        s_new = jnp.sum(iq.astype(F32) * kn, axis=1, keepdims=True)
        val = jnp.sum(jnp.maximum(s_new, 0.0) * w, axis=0, keepdims=True)
        new_ref[0] = jnp.broadcast_to(val, new_ref.shape[1:])


def sample_index_scores(page_table, iq, iw, ik_new, cache_idx_t, layer, *, pg):
    nb, n_pages = page_table.shape
    page = cache_idx_t.shape[3]
    return pl.pallas_call(
        functools.partial(_sample_scores_kernel, pg=pg, page=page),
        grid_spec=pltpu.PrefetchScalarGridSpec(
            num_scalar_prefetch=1,
            grid=(nb, n_pages // pg),
            in_specs=[pl.BlockSpec((1, IDX_HEADS, IDX_DIM), lambda b, pc, pt: (b, 0, 0)),
                      pl.BlockSpec((1, IDX_HEADS, 1), lambda b, pc, pt: (b, 0, 0)),
                      pl.BlockSpec((1, 1, IDX_DIM), lambda b, pc, pt: (b, 0, 0))]
                     + _page_specs(pg, (IDX_DIM, page), layer),
            out_specs=[pl.BlockSpec((1, pg, page), lambda b, pc, pt: (b, pc, 0)),
                       pl.BlockSpec((1, 8, LANE), lambda b, pc, pt: (b, 0, 0))],
        ),
        out_shape=[jax.ShapeDtypeStruct((nb, n_pages, page), F32),
                   jax.ShapeDtypeStruct((nb, 8, LANE), F32)],
        compiler_params=_params("arbitrary", "arbitrary"),
        name="sample_scores",
    )(page_table, iq, iw, ik_new, *([cache_idx_t] * pg))


def _sample_topk_kernel(sc_ref, new_ref, t_ref, x_ref, ns_ref, keys_ref, *, chunk, topk, idx_bits):
    rows, length = sc_ref.shape
    n_chunks = length // chunk

    def fill(c, carry):
        c0 = pl.multiple_of(c * chunk, chunk)
        keys_ref[:, pl.ds(c0, chunk)] = _sortable_key(sc_ref[:, pl.ds(c0, chunk)])
        return carry

    lax.fori_loop(0, n_chunks, fill, 0)
    new_key = _sortable_key(new_ref[:, 0:1])
    t, x, new_sel = _topk_search(keys_ref, n_chunks, chunk, rows, topk, idx_bits, extra_key=new_key)
    t_ref[...] = jnp.broadcast_to(t, t_ref.shape)
    x_ref[...] = jnp.broadcast_to(x, x_ref.shape)
    ns_ref[...] = jnp.broadcast_to(jnp.where(new_sel, 1, 0), ns_ref.shape)


def sample_topk_rule(scores, new_score, *, chunk):
    rows, length = scores.shape
    topk = min(TOPK_MAX, (length + 1) // 4)
    idx_bits = (length - 1).bit_length()
    t, x, ns = pl.pallas_call(
        functools.partial(_sample_topk_kernel, chunk=chunk, topk=topk, idx_bits=idx_bits),
        out_shape=[jax.ShapeDtypeStruct((rows, LANE), I32)] * 3,
        scratch_shapes=[pltpu.VMEM((rows, length), I32)],
        compiler_params=pltpu.CompilerParams(vmem_limit_bytes=VMEM_LIMIT),
        name="sample_topk",
    )(scores, new_score)
    return t[:, 0], x[:, 0], ns[:, 0]


def _sample_compact_kernel(t_ref, x_ref, sc_ref, pt_ref, rows_ref, *, topk, page):
    b = pl.program_id(0)
    n_pages = sc_ref.shape[1]
    t, x = t_ref[b], x_ref[b]
    key = _sortable_key(sc_ref[0])
    col = (lax.broadcasted_iota(I32, key.shape, 0) * page + lax.broadcasted_iota(I32, key.shape, 1))
    sel = jnp.where((key > t) | ((key == t) & (col <= x)), 1.0, 0.0).astype(BF16)
    incl = jnp.where(lax.broadcasted_iota(I32, (page, page), 0) <= lax.broadcasted_iota(I32, (page, page), 1),
                     1.0, 0.0).astype(BF16)
    within = jnp.dot(sel, incl, preferred_element_type=F32)
    per_page = lax.dot_general(jnp.ones((8, page), BF16), sel, _NT, preferred_element_type=F32)
    incl_p = incl if n_pages == page else jnp.where(
        lax.broadcasted_iota(I32, (n_pages, n_pages), 0) <= lax.broadcasted_iota(I32, (n_pages, n_pages), 1),
        1.0, 0.0).astype(BF16)
    upto = jnp.dot(per_page.astype(BF16), incl_p, preferred_element_type=F32)
    upto_row = upto[0:1]
    before_row = upto_row - per_page[0:1]
    total = jnp.max(upto_row, axis=1, keepdims=True)

    j = lax.broadcasted_iota(I32, (topk, 1), 0).astype(F32)
    page_of = jnp.sum(jnp.where(upto_row <= j, 1.0, 0.0), axis=1, keepdims=True)
    onehot = jnp.where(lax.broadcasted_iota(I32, (topk, n_pages), 1).astype(F32) == page_of, 1.0, 0.0)
    within_j = jnp.dot(onehot.astype(BF16), within.astype(BF16), preferred_element_type=F32)
    before_j = jnp.sum(onehot * before_row, axis=1, keepdims=True)
    phys_page = jnp.sum(onehot * pt_ref[0], axis=1, keepdims=True)
    slot = jnp.sum(jnp.where(within_j <= j - before_j, 1.0, 0.0), axis=1, keepdims=True)
    rows = jnp.where(j < total, phys_page * page + slot, 0.0)
    rows_ref[0] = rows.astype(I32)


def sample_compact_rows(t, x, scores, page_table, *, topk):
    nb, n_pages, page = scores.shape
    per_b = lambda b, *_: (b, 0, 0)
    rows = pl.pallas_call(
        functools.partial(_sample_compact_kernel, topk=topk, page=page),
        grid_spec=pltpu.PrefetchScalarGridSpec(
            num_scalar_prefetch=2,
            grid=(nb,),
            in_specs=[pl.BlockSpec((1, n_pages, page), per_b),
                      pl.BlockSpec((1, 1, n_pages), per_b)],
            out_specs=pl.BlockSpec((1, topk, 1), per_b),
        ),
        out_shape=jax.ShapeDtypeStruct((nb, topk, 1), I32),
        compiler_params=_params("arbitrary"),
        name="sample_compact",
    )(t, x, scores, page_table.astype(F32).reshape(nb, 1, n_pages))
    return rows.reshape(nb, topk)


def _online_softmax_update(m_ref, l_ref, acc_ref, s_list, pv):
    m_prev = m_ref[...]
    m_new = m_prev
    for s in s_list:
        m_new = jnp.maximum(m_new, jnp.max(s, axis=1, keepdims=True))
    alpha = jnp.exp(m_prev - m_new)
    l_new = alpha * l_ref[...]
    acc = alpha * acc_ref[...]
    for t, s in enumerate(s_list):
        p = jnp.exp(s - m_new)
        l_new = l_new + jnp.sum(p, axis=1, keepdims=True)
        acc = acc + pv(t, p)
    m_ref[...] = m_new
    l_ref[...] = l_new
    acc_ref[...] = acc


GATHER_UNROLL = 8


def _sample_gather_attn_kernel(rows_ref, ns_ref, q_ref, knew_ref, vnew_ref, gate_ref, ck_hbm, cv_hbm,
                               o_ref, kbuf, vbuf, sem, m_ref, l_ref, acc_ref, *, layer, topk):
    b = pl.program_id(0)
    nb = pl.num_programs(0)
    scale = LANE ** -0.5

    def row_copies(bb, slot, j):
        r = rows_ref[bb, j]
        return (pltpu.make_async_copy(ck_hbm.at[layer, r], kbuf.at[slot, j], sem.at[0, slot]),
                pltpu.make_async_copy(cv_hbm.at[layer, r], vbuf.at[slot, j], sem.at[1, slot]))

    def for_all_rows(bb, slot, action):
        def body(jj, carry):
            for u in range(GATHER_UNROLL):
                for cp in row_copies(bb, slot, jj * GATHER_UNROLL + u):
                    action(cp)
            return carry
        lax.fori_loop(0, topk // GATHER_UNROLL, body, 0)

    slot = b % 2

    @pl.when(b == 0)
    def _():
        for_all_rows(0, 0, lambda cp: cp.start())

    @pl.when(b + 1 < nb)
    def _():
        for_all_rows(b + 1, 1 - slot, lambda cp: cp.start())

    for_all_rows(b, slot, lambda cp: cp.wait())

    q = q_ref[0]
    qf = q.astype(F32)
    new_sel = ns_ref[b]
    n_past = topk - new_sel

    def group_cat(f):
        return jnp.concatenate([f(g) for g in range(A_KV_HEADS)], axis=0)

    m_ref[...] = jnp.full(m_ref.shape, NEG, F32)
    l_ref[...] = jnp.zeros(l_ref.shape, F32)
    acc_ref[...] = jnp.zeros(acc_ref.shape, F32)
    kg = [kbuf[slot, :, g, :].astype(BF16) for g in range(A_KV_HEADS)]
    vg = [vbuf[slot, :, g, :].astype(BF16) for g in range(A_KV_HEADS)]
    s = group_cat(lambda g: lax.dot_general(q[g * A_GROUP:(g + 1) * A_GROUP], kg[g], _NT,
                                            preferred_element_type=F32))
    valid = lax.broadcasted_iota(I32, s.shape, 1) < n_past
    s = jnp.where(valid, s * scale, NEG)

    def pv(_, p):
        pb = p.astype(BF16)
        return group_cat(lambda g: jnp.dot(pb[g * A_GROUP:(g + 1) * A_GROUP], vg[g],
                                           preferred_element_type=F32))

    _online_softmax_update(m_ref, l_ref, acc_ref, [s], pv)

    kn = knew_ref[0].astype(BF16).astype(F32)
    vn = vnew_ref[0].astype(BF16).astype(F32)
    s_new = group_cat(lambda g: jnp.sum(qf[g * A_GROUP:(g + 1) * A_GROUP] * kn[:, g * LANE:(g + 1) * LANE],
                                        axis=1, keepdims=True))
    s_new = s_new * scale + jnp.where(new_sel > 0, 0.0, NEG)

    def pv_new(_, p):
        pf = p.astype(BF16).astype(F32)
        return group_cat(lambda g: pf[g * A_GROUP:(g + 1) * A_GROUP] * vn[:, g * LANE:(g + 1) * LANE])

    _online_softmax_update(m_ref, l_ref, acc_ref, [s_new], pv_new)
    o = acc_ref[...] / l_ref[...]
    o_ref[0] = (o * _silu(gate_ref[0].astype(F32))).astype(o_ref.dtype)


def sample_gather_attention(rows, new_sel, q, k_new, v_new, gate, cache_k, cache_v, layer):
    nb, topk = rows.shape
    kvh, hd = cache_k.shape[2], cache_k.shape[3]
    per_b = lambda b, *_: (b, 0, 0)
    return pl.pallas_call(
        functools.partial(_sample_gather_attn_kernel, layer=layer, topk=topk),
        grid_spec=pltpu.PrefetchScalarGridSpec(
            num_scalar_prefetch=2,
            grid=(nb,),
            in_specs=[pl.BlockSpec((1, A_HEADS, LANE), per_b),
                      pl.BlockSpec((1, 1, kvh * hd), per_b),
                      pl.BlockSpec((1, 1, kvh * hd), per_b),
                      pl.BlockSpec((1, A_HEADS, LANE), per_b),
                      pl.BlockSpec(memory_space=pl.ANY),
                      pl.BlockSpec(memory_space=pl.ANY)],
            out_specs=pl.BlockSpec((1, A_HEADS, LANE), per_b),
            scratch_shapes=[pltpu.VMEM((2, topk, kvh, hd), F32),
                            pltpu.VMEM((2, topk, kvh, hd), F32),
                            pltpu.SemaphoreType.DMA((2, 2)),
                            pltpu.VMEM((A_HEADS, 1), F32), pltpu.VMEM((A_HEADS, 1), F32),
                            pltpu.VMEM((A_HEADS, LANE), F32)],
        ),
        out_shape=jax.ShapeDtypeStruct((nb, A_HEADS, LANE), BF16),
        compiler_params=_params("arbitrary"),
        name="sample_gather_attn",
    )(rows, new_sel, q, k_new, v_new, gate, cache_k, cache_v)


def _head_mm_kernel(x_ref, w_ref, *rest, transpose_w, gated):
    o_ref = rest[-1]
    dims = _NT if transpose_w else (((1,), (0,)), ((), ()))
    y = lax.dot_general(x_ref[...], w_ref[...], dims, preferred_element_type=F32)
    if gated:
        y = y * _silu(rest[0][...].astype(F32))
    o_ref[...] = y.astype(o_ref.dtype)


def absorb_query(qcat, w_kv):
    m = qcat.shape[0]
    c = w_kv.shape[0]
    return pl.pallas_call(
        functools.partial(_head_mm_kernel, transpose_w=True, gated=False),
        grid=(B_HEADS,),
        in_specs=[pl.BlockSpec((m, LANE), lambda h: (0, 2 * h)),
                  pl.BlockSpec((c, LANE), lambda h: (0, 2 * h))],
        out_specs=pl.BlockSpec((m, c), lambda h: (0, h)),
        out_shape=jax.ShapeDtypeStruct((m, B_HEADS * c), BF16),
        compiler_params=_params("arbitrary"),
        name="absorb_query",
    )(qcat, w_kv)


def unabsorb_output(o_lat, w_kv, gate):
    m = o_lat.shape[0]
    c = w_kv.shape[0]
    return pl.pallas_call(
        functools.partial(_head_mm_kernel, transpose_w=False, gated=True),
        grid=(B_HEADS,),
        in_specs=[pl.BlockSpec((m, c), lambda h: (0, h)),
                  pl.BlockSpec((c, LANE), lambda h: (0, 2 * h + 1)),
                  pl.BlockSpec((m, LANE), lambda h: (0, h))],
        out_specs=pl.BlockSpec((m, LANE), lambda h: (0, h)),
        out_shape=jax.ShapeDtypeStruct((m, B_HEADS * LANE), BF16),
        compiler_params=_params("arbitrary"),
        name="unabsorb_output",
    )(o_lat, w_kv, gate)


def _sample_mla_kernel(pt_ref, ql_ref, qp_ref, cnew_ref, knew_ref, *rest, pg):
    cpages, kpages = rest[:pg], rest[pg:2 * pg]
    o_ref, m_ref, l_ref, acc_ref = rest[2 * pg:]
    pc = pl.program_id(1)

    @pl.when(pc == 0)
    def _():
        m_ref[...] = jnp.full(m_ref.shape, NEG, F32)
        l_ref[...] = jnp.zeros(l_ref.shape, F32)
        acc_ref[...] = jnp.zeros(acc_ref.shape, F32)

    ql = ql_ref[0]
    qp = qp_ref[0][:, LANE:LANE + ROPE_DIM]
    c = jnp.concatenate([cp[...] for cp in cpages], axis=0).astype(BF16)
    kpe_t = jnp.concatenate([kp[...] for kp in kpages], axis=1).astype(BF16)
    s = (lax.dot_general(ql, c, _NT, preferred_element_type=F32)
         + jnp.dot(qp, kpe_t, preferred_element_type=F32)) * MLA_SCALE
    _online_softmax_update(m_ref, l_ref, acc_ref, [s],
                           lambda _, p: jnp.dot(p.astype(BF16), c, preferred_element_type=F32))

    @pl.when(pc == pl.num_programs(1) - 1)
    def _():
        cn = cnew_ref[0].astype(BF16).astype(F32)
        kn = knew_ref[0].astype(BF16).astype(F32)
        s_new = (jnp.sum(ql.astype(F32) * cn, axis=1, keepdims=True)
                 + jnp.sum(qp.astype(F32) * kn, axis=1, keepdims=True)) * MLA_SCALE
        _online_softmax_update(m_ref, l_ref, acc_ref, [s_new],
                               lambda _, p: p.astype(BF16).astype(F32) * cn)
        o_ref[0] = (acc_ref[...] / l_ref[...]).astype(o_ref.dtype)


def sample_mla_attention(page_table, q_lat, qcat, c_new, k_new, cache_ckv, cache_kpe_t, layer, *, pg):
    nb, n_pages = page_table.shape
    page, c = cache_ckv.shape[2], cache_ckv.shape[3]
    r = cache_kpe_t.shape[2]
    per_b = lambda b, pc, pt: (b, 0, 0)
    return pl.pallas_call(
        functools.partial(_sample_mla_kernel, pg=pg),
        grid_spec=pltpu.PrefetchScalarGridSpec(
            num_scalar_prefetch=1,
            grid=(nb, n_pages // pg),
            in_specs=[pl.BlockSpec((1, B_HEADS, c), per_b),
                      pl.BlockSpec((1, B_HEADS, 2 * LANE), per_b),
                      pl.BlockSpec((1, 1, c), per_b),
                      pl.BlockSpec((1, 1, r), per_b)]
                     + _page_specs(pg, (page, c), layer) + _page_specs(pg, (r, page), layer),
            out_specs=pl.BlockSpec((1, B_HEADS, c), per_b),
            scratch_shapes=[pltpu.VMEM((B_HEADS, 1), F32), pltpu.VMEM((B_HEADS, 1), F32),
                            pltpu.VMEM((B_HEADS, c), F32)],
        ),
        out_shape=jax.ShapeDtypeStruct((nb, B_HEADS, c), BF16),
        compiler_params=_params("arbitrary", "arbitrary"),
        name="sample_mla_attn",
    )(page_table, q_lat, qcat, c_new, k_new, *([cache_ckv] * pg), *([cache_kpe_t] * pg))


def _rope_tables(pos, head_dim, identity_upper_half=False):
    half = head_dim // 2
    inv_freq = ROPE_THETA ** (-jnp.arange(half, dtype=F32) / half)
    ang = pos.astype(F32)[:, None] * inv_freq[None, :]
    c, s = jnp.cos(ang), jnp.sin(ang)
    cosf = jnp.concatenate([c, c], axis=1)
    sinf = jnp.concatenate([-s, s], axis=1)
    if head_dim < LANE:
        if identity_upper_half:
            cosf = jnp.concatenate([cosf, jnp.ones_like(cosf)], axis=1)
            sinf = jnp.concatenate([sinf, jnp.zeros_like(sinf)], axis=1)
        else:
            cosf = jnp.concatenate([cosf, cosf], axis=1)
            sinf = jnp.concatenate([sinf, sinf], axis=1)
    return cosf, sinf


def _pad_cols(w, n):
    return jnp.pad(w, ((0, 0), (0, n - w.shape[1])))


def _layer_a_weights(w_in, d):
    a_width = A_HEADS * LANE
    kv_width = A_KV_HEADS * LANE
    sizes = (a_width, kv_width, kv_width, IDX_HEADS * IDX_DIM, IDX_DIM, IDX_HEADS, a_width)
    offs = [0]
    for s in sizes:
        offs.append(offs[-1] + s)
    wq, wk, wv, wiq, wik, wiw, wg = (w_in[:, offs[t]:offs[t + 1]] for t in range(7))
    wikw = _pad_cols(jnp.concatenate([wik, wiw], axis=1), LANE)
    return tuple(w.astype(BF16) for w in (wq, wk, wv, wiq, wikw, wg))


def _layer_b_weights(w_in, w_q_b):
    wcq = w_in[:, :Q_LORA]
    wckv = w_in[:, Q_LORA:Q_LORA + KV_LORA]
    wkpe = _pad_cols(w_in[:, Q_LORA + KV_LORA:Q_LORA + KV_LORA + ROPE_DIM], LANE)
    wg = w_in[:, Q_LORA + KV_LORA + ROPE_DIM:]
    wq = jnp.pad(w_q_b.reshape(Q_LORA, B_HEADS, NOPE_DIM + ROPE_DIM),
                 ((0, 0), (0, 0), (0, 2 * LANE - NOPE_DIM - ROPE_DIM))).reshape(Q_LORA, B_HEADS * 2 * LANE)
    return tuple(w.astype(BF16) for w in (wcq, wckv, wkpe, wg, wq))


def _project_a(h, wa, tabs, tm):
    wq, wk, wv, wiq, wikw, wg = wa
    t128, t64, t64id = tabs
    mm = functools.partial(matmul_epi, h, tm=tm)
    (q,) = mm(wq, [BF16], tn=512, mode="rope", tables=t128, name="a_q")
    k32, k16 = mm(wk, [F32, BF16], tn=256, mode="rope", tables=t128, name="a_k")
    v32, v16 = mm(wv, [F32, BF16], tn=256, name="a_v")
    (iq,) = mm(wiq, [BF16], tn=512, mode="rope", tables=t64, head_dim=IDX_DIM, name="a_iq")
    (ikw,) = mm(wikw, [F32], tn=LANE, mode="rope", tables=t64id, head_dim=IDX_DIM, name="a_ikw")
    (gate,) = mm(wg, [BF16], tn=512, name="a_gate")
    return q, k32, k16, v32, v16, iq, ikw, gate


def _project_b(h, wb, q_a_norm, kv_a_norm, t64id, tm):
    wcq, wckv, wkpe, wg, wq = wb
    mm = functools.partial(matmul_epi, tm=tm)
    (cq,) = mm(h, wcq, [BF16], tn=Q_LORA, mode="rmsnorm", gain=q_a_norm, name="b_cq")
    ckv32, ckv16 = mm(h, wckv, [F32, BF16], tn=KV_LORA, mode="rmsnorm", gain=kv_a_norm, name="b_ckv")
    kpe32, kpe16 = mm(h, wkpe, [F32, BF16], tn=LANE, mode="rope", tables=t64id, head_dim=ROPE_DIM,
                      name="b_kpe")
    (gate,) = mm(h, wg, [BF16], tn=512, name="b_gate")
    (qcat,) = matmul_epi(cq, wq, [BF16], tm=2 * tm, tn=1024, mode="rope", tables=t64id, head_dim=ROPE_DIM,
                         rope_slabs="odd", name="b_q")
    return qcat, ckv32, ckv16, kpe32, kpe16, gate


def kernel(x_prompt, x_sample, cache_a_k, cache_a_v, cache_a_idx, cache_b_ckv, cache_b_kpe, page_table,
           norm_a, w_in_a, w_out_a, norm_b, w_in_b, q_a_norm_b, w_q_b, kv_a_norm_b, w_kv_b, w_out_b,
           final_norm):
    batch, seq, d = x_prompt.shape
    nb = x_sample.shape[0]
    n_pages = page_table.shape[1]
    n_layers_a, n_pool, page = cache_a_k.shape[:3]
    past = n_pages * page
    mp = batch * seq
    tm_p = min(512, seq)
    kv_width = A_KV_HEADS * LANE

    xp = x_prompt.reshape(mp, d)
    xs = x_sample.reshape(nb, d)
    pos_p = jnp.arange(seq, dtype=I32)
    pos_s = jnp.full((nb,), past, dtype=I32)
    tabs_p = (_rope_tables(pos_p, LANE), _rope_tables(pos_p, IDX_DIM), _rope_tables(pos_p, IDX_DIM, True))
    tabs_s = (_rope_tables(pos_s, LANE), _rope_tables(pos_s, IDX_DIM), _rope_tables(pos_s, IDX_DIM, True))

    wa = _layer_a_weights(w_in_a[0], d)
    wb = _layer_b_weights(w_in_b[0], w_q_b[0])
    w_out_a16 = w_out_a[0].astype(BF16)
    w_out_b16 = w_out_b[0].astype(BF16)
    w_kv16 = w_kv_b[0].astype(BF16)

    cache_idx_t = jnp.swapaxes(cache_a_idx, 2, 3)
    cache_kpe_t = jnp.swapaxes(cache_b_kpe, 2, 3)
    cache_k_rows = cache_a_k.reshape(n_layers_a, n_pool * page, A_KV_HEADS, LANE)
    cache_v_rows = cache_a_v.reshape(n_layers_a, n_pool * page, A_KV_HEADS, LANE)

    hs = rmsnorm_cast(xs, norm_a[0], nb)
    q, sk32, _, sv32, _, iq, s_ikw, gate = _project_a(hs, wa, tabs_s, nb)
    scores, new_score = sample_index_scores(
        page_table, iq.reshape(nb, IDX_HEADS, IDX_DIM),
        s_ikw[:, IDX_DIM:IDX_DIM + IDX_HEADS].reshape(nb, IDX_HEADS, 1),
        s_ikw[:, :IDX_DIM].reshape(nb, 1, IDX_DIM), cache_idx_t, 0, pg=min(32, n_pages))
    t, x, new_sel = sample_topk_rule(scores.reshape(nb, past), new_score[:, 0, :], chunk=min(512, past))
    rows = sample_compact_rows(t, x, scores, page_table, topk=min(TOPK_MAX, (past + 1) // 4))
    og = sample_gather_attention(rows, new_sel, q.reshape(nb, A_HEADS, LANE),
                                 sk32.reshape(nb, 1, kv_width), sv32.reshape(nb, 1, kv_width),
                                 gate.reshape(nb, A_HEADS, LANE), cache_k_rows, cache_v_rows, 0)
    xs, hs = outproj_residual_norm(og.reshape(nb, A_HEADS * LANE), w_out_a16, xs, norm_b[0], tm=nb,
                                   final=False)

    hp = rmsnorm_cast(xp, norm_a[0], tm_p)
    q, pk32, k16, pv32, v16, iq, p_ikw, gate = _project_a(hp, wa, tabs_p, tm_p)
    og = dsa_prompt_attention(iq, p_ikw, q, k16, v16, gate, batch=batch, seq=seq,
                              tq=min(256, seq), chunk=min(512, seq))
    xp, hp = outproj_residual_norm(og, w_out_a16, xp, norm_b[0], tm=256, final=False)

    qcat, s_ckv32, _, s_kpe32, _, gate = _project_b(hs, wb, q_a_norm_b[0], kv_a_norm_b[0], tabs_s[2], nb)
    q_lat = absorb_query(qcat, w_kv16)
    o_lat = sample_mla_attention(page_table, q_lat.reshape(nb, B_HEADS, KV_LORA),
                                 qcat.reshape(nb, B_HEADS, 2 * LANE),
                                 s_ckv32.reshape(nb, 1, KV_LORA),
                                 s_kpe32[:, :ROPE_DIM].reshape(nb, 1, ROPE_DIM),
                                 cache_b_ckv, cache_kpe_t, 0, pg=min(32, n_pages))
    og = unabsorb_output(o_lat.reshape(nb, B_HEADS * KV_LORA), w_kv16, gate)
    (y_sample,) = outproj_residual_norm(og, w_out_b16, xs, final_norm, tm=nb, final=True)

    qcat, p_ckv32, ckv16, p_kpe32, kpe16, gate = _project_b(hp, wb, q_a_norm_b[0], kv_a_norm_b[0],
                                                             tabs_p[2], tm_p)
    (kv,) = matmul_epi(ckv16, w_kv16, [BF16], tm=2 * tm_p, tn=1024, name="b_kv")
    og = mla_prompt_attention(qcat, kv, kpe16, gate, batch=batch, seq=seq, tq=min(512, seq))
    (y_prompt,) = outproj_residual_norm(og, w_out_b16, xp, final_norm, tm=256, final=True)

    return (y_prompt.reshape(batch, seq, d), y_sample.reshape(nb, 1, d),
            pk32.reshape(1, batch, seq, A_KV_HEADS, LANE), pv32.reshape(1, batch, seq, A_KV_HEADS, LANE),
            p_ikw[:, :IDX_DIM].reshape(1, batch, seq, IDX_DIM),
            p_ckv32.reshape(1, batch, seq, KV_LORA), p_kpe32[:, :ROPE_DIM].reshape(1, batch, seq, ROPE_DIM),
            sk32.reshape(1, nb, 1, A_KV_HEADS, LANE), sv32.reshape(1, nb, 1, A_KV_HEADS, LANE),
            s_ikw[:, :IDX_DIM].reshape(1, nb, 1, IDX_DIM),
            s_ckv32.reshape(1, nb, 1, KV_LORA), s_kpe32[:, :ROPE_DIM].reshape(1, nb, 1, ROPE_DIM))
```
